```python
import math
import jax, jax.numpy as jnp
from jax import lax
import numpy as np

D_MODEL = 1024
BATCH = 2
SEQ = 16384
DEPTH = 4

N_MIXERS = 3
EPS = 1e-6
ATTN_GROUPS = ((128, 1), (512, 4), (2048, 16))
ATTN_N_GROUPS = len(ATTN_GROUPS)
ATTN_HEADS_PER_GROUP = 8
ATTN_HEAD_DIM = 64
ATTN_BLOCK = 128
ATTN_GROUP_WIDTH = ATTN_HEADS_PER_GROUP * ATTN_HEAD_DIM
ATTN_IN_WIDTH = ATTN_N_GROUPS * 3 * ATTN_GROUP_WIDTH
CONV_CHANNELS = D_MODEL
CONV_WIDTH = 31
HGRN_EXPAND = 128
HGRN_HEADS = D_MODEL // HGRN_EXPAND
HGRN_KEY_DIM = HGRN_HEADS * HGRN_EXPAND
HGRN_HEAD_V = D_MODEL // HGRN_HEADS
HGRN_VALUE_DIM = HGRN_HEADS * HGRN_HEAD_V
HGRN_CHUNK = 64
FFN_HIDDEN = 2816
FFN_CONV_WIDTH = 3
N_ATTN_LAYERS = len(range(0, DEPTH, N_MIXERS))
N_CONV_LAYERS = len(range(1, DEPTH, N_MIXERS))
N_HGRN_LAYERS = len(range(2, DEPTH, N_MIXERS))

kernel_name = "hybrid_dilated_attn_conformer_hgrn2_trunk"


def rms_norm(x, gain):
    xf = x.astype(jnp.float32)
    y = xf * lax.rsqrt(jnp.mean(xf * xf, axis=-1, keepdims=True) + EPS)
    return (y * gain.astype(jnp.float32)).astype(x.dtype)


def layer_norm(x, gain, bias):
    xf = x.astype(jnp.float32)
    mu = jnp.mean(xf, axis=-1, keepdims=True)
    var = jnp.mean(jnp.square(xf - mu), axis=-1, keepdims=True)
    y = (xf - mu) * lax.rsqrt(var + EPS)
    return (y * gain.astype(jnp.float32) + bias.astype(jnp.float32)).astype(x.dtype)


def causal_depthwise_conv(x, w, b):
    K, C = w.shape
    y = lax.conv_general_dilated(
        x, w[:, None, :].astype(x.dtype), window_strides=(1,), padding=((K - 1, 0),),
        dimension_numbers=('NWC', 'WIO', 'NWC'), feature_group_count=C)
    return y + b.astype(x.dtype)


def dilated_window_attention(q, k, v, window, dilation):
    B, S, H, Dh = q.shape
    span = window // dilation
    unit = dilation * ATTN_BLOCK
    L = -(-S // unit) * unit
    n = L // dilation
    nb = n // ATTN_BLOCK

    def to_blocks(t):
        t = jnp.pad(t, ((0, 0), (0, L - S), (0, 0), (0, 0)))
        t = t.reshape(B, n, dilation, H, Dh).transpose(0, 2, 3, 1, 4)
        return t.reshape(B, dilation, H, nb, ATTN_BLOCK, Dh)

    def with_prev(t):
        prev = jnp.pad(t, ((0, 0), (0, 0), (0, 0), (1, 0), (0, 0), (0, 0)))[:, :, :, :-1]
        return jnp.concatenate([prev, t], axis=-2)

    qb = to_blocks(q)
    kw = with_prev(to_blocks(k))
    vw = with_prev(to_blocks(v))
    s = jnp.einsum('brhnqd,brhnkd->brhnqk', qb, kw).astype(jnp.float32)
    qi = jnp.arange(ATTN_BLOCK)[:, None]
    kj = jnp.arange(2 * ATTN_BLOCK)[None, :]
    dist = qi + ATTN_BLOCK - kj
    band = (dist >= 0) & (dist <= span)
    kpos = jnp.arange(nb)[:, None, None] * ATTN_BLOCK + kj[None] - ATTN_BLOCK
    mask = band[None] & (kpos >= 0)
    s = jnp.where(mask, s, -jnp.inf)
    lse = jax.nn.logsumexp(s, axis=-1)
    p = jnp.exp(s - lse[..., None])
    o = jnp.einsum('brhnqk,brhnkd->brhnqd', p.astype(v.dtype), vw)
    o = o.reshape(B, dilation, H, n, Dh).transpose(0, 3, 1, 2, 4).reshape(B, L, H, Dh)[:, :S]
    lse = lse.reshape(B, dilation, H, n).transpose(0, 3, 1, 2).reshape(B, L, H)[:, :S]
    return o, lse


def dilated_attention_mixer(h, w_in, q_gain, k_gain, w_out):
    B, S, _ = h.shape
    proj = (h @ w_in).reshape(B, S, ATTN_N_GROUPS, 3, ATTN_HEADS_PER_GROUP, ATTN_HEAD_DIM)
    scale = ATTN_HEAD_DIM ** -0.5
    outs, lses = [], []
    for g, (window, dilation) in enumerate(ATTN_GROUPS):
        q = rms_norm(proj[:, :, g, 0], q_gain[g]) * scale
        k = rms_norm(proj[:, :, g, 1], k_gain[g])
        o, lse = dilated_window_attention(q, k, proj[:, :, g, 2], window, dilation)
        outs.append(o)
        lses.append(lse)
    weights = jax.nn.softmax(jnp.stack(lses), axis=0)
    o = jnp.sum(weights[..., None] * jnp.stack(outs).astype(jnp.float32), axis=0)
    return o.astype(h.dtype).reshape(B, S, ATTN_GROUP_WIDTH) @ w_out


def conformer_conv_mixer(h, w_in, b_in, dw_w, dw_b, ln_g, ln_b, w_out, b_out):
    u = h @ w_in + b_in
    a, gate = jnp.split(u, 2, axis=-1)
    u = a * jax.nn.sigmoid(gate)
    u = causal_depthwise_conv(u, dw_w, dw_b)
    u = jax.nn.silu(layer_norm(u, ln_g, ln_b))
    return u @ w_out + b_out


def hgrn2_chunk_scan(q, k, v, log_f):
    B, S, H, K = q.shape
    V = v.shape[-1]
    C = HGRN_CHUNK
    n = S // C

    def chunks(t):
        return t.reshape(B, n, C, H, t.shape[-1]).transpose(1, 0, 3, 2, 4)

    qc, kc, vc = chunks(q), chunks(k), chunks(v)
    bc = jnp.cumsum(chunks(log_f), axis=-2)
    tri = jnp.tril(jnp.ones((C, C), dtype=bool))[:, :, None]

    def step(state, xs):
        q_, k_, v_, b_ = xs
        o_inter = jnp.einsum('bhtk,bhkv->bhtv', q_ * jnp.exp(b_), state)
        diff = b_[:, :, :, None, :] - b_[:, :, None, :, :]
        decay = jnp.where(tri, jnp.exp(jnp.where(tri, diff, 0.0)), 0.0)
        scores = jnp.einsum('bhtsk,bhsk->bhts', q_[:, :, :, None, :] * decay, k_)
        o_intra = jnp.einsum('bhts,bhsv->bhtv', scores, v_)
        b_last = b_[:, :, -1, :]
        k_dec = k_ * jnp.exp(b_last[:, :, None, :] - b_)
        state = state * jnp.exp(b_last)[..., None] + jnp.einsum('bhsk,bhsv->bhkv', k_dec, v_)
        return state, o_inter + o_intra

    state0 = jnp.zeros((B, H, K, V), jnp.float32)
    _, o = lax.scan(step, state0, (qc, kc, vc, bc))
    return o.transpose(1, 0, 3, 2, 4).reshape(B, S, H, V)


def hgrn2_mixer(h, w_in, lower_bound, norm_gain, w_out):
    B, S, _ = h.shape
    q, f, i, g = jnp.split(h @ w_in, [HGRN_KEY_DIM, 2 * HGRN_KEY_DIM,
                                      2 * HGRN_KEY_DIM + HGRN_VALUE_DIM], axis=-1)
    q = jax.nn.silu(q.astype(jnp.float32))
    lb = lower_bound.astype(jnp.float32)
    log_f = jnp.logaddexp(jnp.log(lb), jnp.log1p(-lb) + jax.nn.log_sigmoid(f.astype(jnp.float32)))
    k = -jnp.expm1(log_f)
    heads = lambda t, dim: t.reshape(B, S, HGRN_HEADS, dim)
    o = hgrn2_chunk_scan(heads(q, HGRN_EXPAND), heads(k, HGRN_EXPAND),
                         heads(i.astype(jnp.float32), HGRN_HEAD_V), heads(log_f, HGRN_EXPAND))
    o = rms_norm(o, norm_gain.reshape(HGRN_HEADS, HGRN_HEAD_V)) * \
        jax.nn.silu(heads(g.astype(jnp.float32), HGRN_HEAD_V))
    return o.reshape(B, S, HGRN_VALUE_DIM).astype(h.dtype) @ w_out


def conv_ffn(h, w_up, conv_w, conv_b, w_down):
    u = causal_depthwise_conv(h @ w_up, conv_w, conv_b)
    gate, up = jnp.split(u, 2, axis=-1)
    return (jax.nn.silu(gate) * up) @ w_down


def setup_inputs(seed: int = 0) -> dict:
    key = jax.random.key(seed)
    ks = iter(jax.random.split(key, 32))
    D = D_MODEL
    F2 = 2 * FFN_HIDDEN

    def nrm(shape, scale):
        return scale * jax.random.normal(next(ks), shape, jnp.float32)

    def gain(shape):
        return 1.0 + nrm(shape, 0.02)

    return {
        "x": nrm((BATCH, SEQ, D), 1.0),
        "mixer_norm": gain((DEPTH, D)),
        "ffn_norm": gain((DEPTH, D)),
        "attn_w_in": nrm((N_ATTN_LAYERS, D, ATTN_IN_WIDTH), D ** -0.5),
        "attn_q_gain": gain((N_ATTN_LAYERS, ATTN_N_GROUPS, ATTN_HEAD_DIM)),
        "attn_k_gain": gain((N_ATTN_LAYERS, ATTN_N_GROUPS, ATTN_HEAD_DIM)),
        "attn_w_out": nrm((N_ATTN_LAYERS, ATTN_GROUP_WIDTH, D), ATTN_GROUP_WIDTH ** -0.5),
        "conv_w_in": nrm((N_CONV_LAYERS, D, 2 * CONV_CHANNELS), D ** -0.5),
        "conv_b_in": nrm((N_CONV_LAYERS, 2 * CONV_CHANNELS), 0.02),
        "conv_dw_w": nrm((N_CONV_LAYERS, CONV_WIDTH, CONV_CHANNELS), CONV_WIDTH ** -0.5),
        "conv_dw_b": nrm((N_CONV_LAYERS, CONV_CHANNELS), 0.02),
        "conv_ln_g": gain((N_CONV_LAYERS, CONV_CHANNELS)),
        "conv_ln_b": nrm((N_CONV_LAYERS, CONV_CHANNELS), 0.02),
        "conv_w_out": nrm((N_CONV_LAYERS, CONV_CHANNELS, D), CONV_CHANNELS ** -0.5),
        "conv_b_out": nrm((N_CONV_LAYERS, D), 0.02),
        "hgrn_w_in": nrm((N_HGRN_LAYERS, D, 2 * HGRN_KEY_DIM + 2 * HGRN_VALUE_DIM), D ** -0.5),
        "hgrn_lb_logits": nrm((DEPTH, HGRN_KEY_DIM), 1.0),
        "hgrn_norm_g": gain((N_HGRN_LAYERS, HGRN_VALUE_DIM)),
        "hgrn_w_out": nrm((N_HGRN_LAYERS, HGRN_VALUE_DIM, D), HGRN_VALUE_DIM ** -0.5),
        "ffn_w_up": nrm((DEPTH, D, F2), D ** -0.5),
        "ffn_conv_w": nrm((DEPTH, FFN_CONV_WIDTH, F2), FFN_CONV_WIDTH ** -0.5),
        "ffn_conv_b": nrm((DEPTH, F2), 0.02),
        "ffn_w_down": nrm((DEPTH, FFN_HIDDEN, D), FFN_HIDDEN ** -0.5),
    }


def reference(x, mixer_norm, ffn_norm, attn_w_in, attn_q_gain, attn_k_gain, attn_w_out,
              conv_w_in, conv_b_in, conv_dw_w, conv_dw_b, conv_ln_g, conv_ln_b, conv_w_out,
              conv_b_out, hgrn_w_in, hgrn_lb_logits, hgrn_norm_g, hgrn_w_out,
              ffn_w_up, ffn_conv_w, ffn_conv_b, ffn_w_down):
    lb_cum = jnp.cumsum(jax.nn.softmax(hgrn_lb_logits.astype(jnp.float32), axis=0), axis=0)
    lower_bounds = lb_cum - lb_cum[0]
    for layer in range(DEPTH):
        kind = layer % N_MIXERS
        j = layer // N_MIXERS
        h = rms_norm(x, mixer_norm[layer])
        if kind == 0:
            mix = dilated_attention_mixer(h, attn_w_in[j], attn_q_gain[j], attn_k_gain[j],
                                          attn_w_out[j])
        elif kind == 1:
            mix = conformer_conv_mixer(h, conv_w_in[j], conv_b_in[j], conv_dw_w[j], conv_dw_b[j],
                                       conv_ln_g[j], conv_ln_b[j], conv_w_out[j], conv_b_out[j])
        else:
            mix = hgrn2_mixer(h, hgrn_w_in[j], lower_bounds[layer], hgrn_norm_g[j], hgrn_w_out[j])
        x = x + mix
        x = x + conv_ffn(rms_norm(x, ffn_norm[layer]), ffn_w_up[layer], ffn_conv_w[layer],
                         ffn_conv_b[layer], ffn_w_down[layer])
    return x
```

```python
import functools

import jax
import jax.numpy as jnp
from jax import lax
from jax.experimental import pallas as pl
from jax.experimental.pallas import tpu as pltpu

F32 = jnp.float32
BF16 = jnp.bfloat16

NORM_EPS = 1e-6
N_MIXERS = 3

ATTN_GROUPS = ((128, 1), (512, 4), (2048, 16))
ATTN_HEADS = 8
ATTN_HEAD_DIM = 64
ATTN_BLOCK = 128
ATTN_WIDTH = ATTN_HEADS * ATTN_HEAD_DIM
ATTN_PARTS = 3 * len(ATTN_GROUPS)
ATTN_Q_ROWS = 512

HGRN_HEADS = 8
HGRN_HEAD_DIM = 128
HGRN_CHUNK = 16
HGRN_ROWS = 256

CONV_ROW_BLOCK = 64
CONV_LANE_BLOCK = 128
CONV_HALO = 32

FFN_CHUNK = 256
FFN_HALO = 8
ROW_TILE = 512

VMEM_LIMIT = 56 * 1024 * 1024


def _params(n_axes):
    return pltpu.CompilerParams(dimension_semantics=("arbitrary",) * n_axes,
                                vmem_limit_bytes=VMEM_LIMIT)


def _resident(shape):
    zeros = (0,) * len(shape)
    return pl.BlockSpec(shape, lambda *_: zeros, pipeline_mode=pl.Buffered(1))


def _rms_norm(x, gain):
    ms = jnp.mean(x * x, axis=-1, keepdims=True)
    return x * lax.rsqrt(ms + NORM_EPS) * gain


def _sigmoid(x):
    return 1.0 / (1.0 + jnp.exp(-x))


def _shift_rows(u, prev, k):
    r = pltpu.roll(u, k, 0)
    p = pltpu.roll(prev, k, 0)
    row = lax.broadcasted_iota(jnp.int32, prev.shape, 0)
    head = jnp.where(row < k, p, r[:8])
    return jnp.concatenate([head, r[8:]], axis=0)


def _ffn_kernel(x_ref, g_ref, wup_ref, cw_ref, cb_ref, wdn_ref, o_ref, halo_ref, acc_ref, *, nc, ch):
    @pl.when(pl.program_id(1) == 0)
    def _():
        halo_ref[...] = jnp.zeros_like(halo_ref)

    x = x_ref[0]
    tm = x.shape[0]
    hb = _rms_norm(x, g_ref[...]).astype(BF16)
    acc_ref[...] = x

    def chunk(c, carry):
        u = jnp.dot(hb, wup_ref[c], preferred_element_type=F32)
        prev = halo_ref[c]
        cw = cw_ref[c]
        y = (u * cw[2:3] + _shift_rows(u, prev, 1) * cw[1:2] + _shift_rows(u, prev, 2) * cw[0:1]
             + cb_ref[c])
        halo_ref[c] = u[tm - FFN_HALO:]
        gate = y[:, :ch]
        act = (gate * _sigmoid(gate) * y[:, ch:]).astype(BF16)
        acc_ref[...] += jnp.dot(act, wdn_ref[c], preferred_element_type=F32)
        return carry

    lax.fori_loop(0, nc, chunk, 0)
    o_ref[0] = acc_ref[...]


def _ffn(x, gain, w_up, conv_w, conv_b, w_down):
    B, S, D = x.shape
    F = w_down.shape[0]
    ch = FFN_CHUNK
    nc = F // ch
    assert nc * ch == F and S % ROW_TILE == 0
    wup = w_up.astype(BF16).reshape(D, 2, nc, ch).transpose(2, 0, 1, 3).reshape(nc, D, 2 * ch)
    cw = conv_w.reshape(3, 2, nc, ch).transpose(2, 0, 1, 3).reshape(nc, 3, 2 * ch)
    cb = conv_b.reshape(2, nc, ch).transpose(1, 0, 2).reshape(nc, 1, 2 * ch)
    wdn = w_down.astype(BF16).reshape(nc, ch, D)
    tile = pl.BlockSpec((1, ROW_TILE, D), lambda b, i: (b, i, 0))
    return pl.pallas_call(
        functools.partial(_ffn_kernel, nc=nc, ch=ch),
        grid=(B, S // ROW_TILE),
        in_specs=[tile, _resident((1, D)), _resident(wup.shape), _resident(cw.shape),
                  _resident(cb.shape), _resident(wdn.shape)],
        out_specs=tile,
        out_shape=jax.ShapeDtypeStruct(x.shape, x.dtype),
        scratch_shapes=[pltpu.VMEM((nc, FFN_HALO, 2 * ch), F32), pltpu.VMEM((ROW_TILE, D), F32)],
        compiler_params=_params(2),
        name="ffn",
    )(x, gain.reshape(1, D), wup, cw, cb, wdn)


def _attn_in_kernel(x_ref, g_ref, w_ref, qk_ref, bd_ref, o_ref):
    hb = _rms_norm(x_ref[0], g_ref[...]).astype(BF16)
    bd = bd_ref[...]
    half = bd.shape[0]
    for j in range(ATTN_PARTS):
        cols = slice(j * ATTN_WIDTH, (j + 1) * ATTN_WIDTH)
        acc = jnp.dot(hb, w_ref[:, cols], preferred_element_type=F32)
        if j % 3 != 2:
            sq = (acc * acc).astype(BF16)
            ssq = jnp.concatenate(
                [jnp.dot(sq[:, :half], bd, preferred_element_type=F32),
                 jnp.dot(sq[:, half:], bd, preferred_element_type=F32)], axis=1)
            acc = acc * lax.rsqrt(ssq * (1.0 / ATTN_HEAD_DIM) + NORM_EPS) * qk_ref[j:j + 1, :]
        o_ref[0, :, cols] = acc.astype(BF16)


def _attn_in(x, gain, w_in, q_gain, k_gain):
    B, S, D = x.shape
    W = w_in.shape[1]
    scale = ATTN_HEAD_DIM ** -0.5
    rows = []
    for g in range(len(ATTN_GROUPS)):
        rows += [jnp.tile(q_gain[g], ATTN_HEADS) * scale, jnp.tile(k_gain[g], ATTN_HEADS),
                 jnp.ones((ATTN_WIDTH,), F32)]
    qk = jnp.stack(rows).astype(F32)
    half = ATTN_WIDTH // 2
    head = jnp.arange(half) // ATTN_HEAD_DIM
    bd = (head[:, None] == head[None, :]).astype(BF16)
    return pl.pallas_call(
        _attn_in_kernel,
        grid=(B, S // ROW_TILE),
        in_specs=[pl.BlockSpec((1, ROW_TILE, D), lambda b, i: (b, i, 0)), _resident((1, D)),
                  _resident((D, W)), _resident(qk.shape), _resident(bd.shape)],
        out_specs=pl.BlockSpec((1, ROW_TILE, W), lambda b, i: (b, i, 0)),
        out_shape=jax.ShapeDtypeStruct((B, S, W), BF16),
        compiler_params=_params(2),
        name="attn_in",
    )(x, gain.reshape(1, D), w_in.astype(BF16), qk, bd)


def _attn_core_kernel(q_ref, k_ref, v_ref, bias_ref, o_ref, lse_ref, kbuf, vbuf):
    blk = ATTN_BLOCK
    nq = q_ref.shape[1]
    first = pl.program_id(2) == 0

    @pl.when(first)
    def _():
        kbuf[0:blk] = jnp.zeros((blk, ATTN_WIDTH), BF16)
        vbuf[0:blk] = jnp.zeros((blk, ATTN_WIDTH), BF16)

    kbuf[blk:] = k_ref[0]
    vbuf[blk:] = v_ref[0]
    lo = lax.broadcasted_iota(jnp.int32, (1, 2 * ATTN_HEAD_DIM), 1) < ATTN_HEAD_DIM
    zero = jnp.zeros((), BF16)
    nt = (((1,), (1,)), ((), ()))
    for jq in range(nq // blk):
        bias = bias_ref[1]
        if jq == 0:
            bias = jnp.where(first, bias_ref[0], bias)
        rows = slice(jq * blk, (jq + 1) * blk)
        for hp in range(ATTN_HEADS // 2):
            lanes = slice(hp * 2 * ATTN_HEAD_DIM, (hp + 1) * 2 * ATTN_HEAD_DIM)
            q2 = q_ref[0, rows, lanes]
            kk = kbuf[jq * blk:(jq + 2) * blk, lanes]
            vv = vbuf[jq * blk:(jq + 2) * blk, lanes]
            qq = jnp.concatenate([jnp.where(lo, q2, zero), jnp.where(lo, zero, q2)], axis=0)
            s2 = lax.dot_general(qq, kk, nt, preferred_element_type=F32)
            parts = []
            for e in range(2):
                s = s2[e * blk:(e + 1) * blk] + bias
                m = jnp.max(s, axis=-1, keepdims=True)
                p = jnp.exp(s - m)
                l = jnp.sum(p, axis=-1, keepdims=True)
                parts.append((p.astype(BF16), 1.0 / l, m + jnp.log(l)))
            pcat = jnp.concatenate([parts[0][0], parts[1][0]], axis=1)
            vcat = jnp.concatenate([jnp.where(lo, vv, zero), jnp.where(lo, zero, vv)], axis=0)
            o2 = jnp.dot(pcat, vcat, preferred_element_type=F32)
            o_ref[0, rows, lanes] = (o2 * jnp.where(lo, parts[0][1], parts[1][1])).astype(o_ref.dtype)
            lse_ref[0, rows, lanes] = jnp.where(lo, parts[0][2], parts[1][2])
    kbuf[0:blk] = kbuf[nq:nq + blk]
    vbuf[0:blk] = vbuf[nq:nq + blk]


def _attn_core(qkv, group, window, dilation):
    B, S, W = qkv.shape
    d = dilation
    blk = ATTN_BLOCK
    assert window // d == blk and S % (d * ATTN_Q_ROWS) == 0
    n = S // d
    view = qkv.reshape(B, n, d * W)
    parts_per_tok = W // ATTN_WIDTH
    qi = jnp.arange(blk)[:, None]
    kj = jnp.arange(2 * blk)[None, :]
    dist = qi + blk - kj
    band = (dist >= 0) & (dist <= blk)
    neg = jnp.float32(-jnp.inf)
    bias = jnp.stack([jnp.where(band & (kj >= blk), 0.0, neg), jnp.where(band, 0.0, neg)]).astype(F32)

    def spec(part):
        return pl.BlockSpec((1, ATTN_Q_ROWS, ATTN_WIDTH),
                            lambda b, r, i: (b, i, r * parts_per_tok + 3 * group + part))

    out_spec = pl.BlockSpec((1, ATTN_Q_ROWS, ATTN_WIDTH), lambda b, r, i: (b, i, r))
    o, lse = pl.pallas_call(
        _attn_core_kernel,
        grid=(B, d, n // ATTN_Q_ROWS),
        in_specs=[spec(0), spec(1), spec(2), _resident(bias.shape)],
        out_specs=[out_spec, out_spec],
        out_shape=[jax.ShapeDtypeStruct((B, n, d * ATTN_WIDTH), BF16),
                   jax.ShapeDtypeStruct((B, n, d * ATTN_WIDTH), F32)],
        scratch_shapes=[pltpu.VMEM((ATTN_Q_ROWS + blk, ATTN_WIDTH), BF16),
                        pltpu.VMEM((ATTN_Q_ROWS + blk, ATTN_WIDTH), BF16)],
        compiler_params=_params(3),
        name=f"attn_core_d{d}",
    )(view, view, view, bias)
    return o.reshape(B, S, ATTN_WIDTH), lse.reshape(B, S, ATTN_WIDTH)


def _attn_out_kernel(x_ref, o0, o1, o2, l0, l1, l2, w_ref, out_ref):
    ls = [l0[0], l1[0], l2[0]]
    m = jnp.maximum(jnp.maximum(ls[0], ls[1]), ls[2])
    es = [jnp.exp(l - m) for l in ls]
    inv = 1.0 / (es[0] + es[1] + es[2])
    mix = (es[0] * o0[0].astype(F32) + es[1] * o1[0].astype(F32) + es[2] * o2[0].astype(F32)) * inv
    out_ref[0] = x_ref[0] + jnp.dot(mix.astype(BF16), w_ref[...], preferred_element_type=F32)


def _attn_out(x, outs, lses, w_out):
    B, S, D = x.shape
    tile = pl.BlockSpec((1, ROW_TILE, D), lambda b, i: (b, i, 0))
    part = pl.BlockSpec((1, ROW_TILE, ATTN_WIDTH), lambda b, i: (b, i, 0))
    return pl.pallas_call(
        _attn_out_kernel,
        grid=(B, S // ROW_TILE),
        in_specs=[tile] + [part] * 6 + [_resident(w_out.shape)],
        out_specs=tile,
        out_shape=jax.ShapeDtypeStruct(x.shape, x.dtype),
        compiler_params=_params(2),
        name="attn_out",
    )(x, *outs, *lses, w_out.astype(BF16))


def _attention_mixer(x, gain, w_in, q_gain, k_gain, w_out):
    qkv = _attn_in(x, gain, w_in, q_gain, k_gain)
    outs, lses = [], []
    for g, (window, dilation) in enumerate(ATTN_GROUPS):
        o, lse = _attn_core(qkv, g, window, dilation)
        outs.append(o)
        lses.append(lse)
    return _attn_out(x, outs, lses, w_out)


def _conv_kernel(x_ref, g_ref, win_ref, bin_ref, dw_ref, dwb_ref, lng_ref, lnb_ref, wout_ref, bout_ref,
                 o_ref, gbuf, ybuf, *, width):
    halo = CONV_HALO
    tm = x_ref.shape[1]
    C = gbuf.shape[1]

    @pl.when(pl.program_id(1) == 0)
    def _():
        gbuf[0:halo] = jnp.zeros((halo, C), F32)

    x = x_ref[0]
    hb = _rms_norm(x, g_ref[...]).astype(BF16)
    a = jnp.dot(hb, win_ref[:, :C], preferred_element_type=F32) + bin_ref[:, :C]
    gate = jnp.dot(hb, win_ref[:, C:], preferred_element_type=F32) + bin_ref[:, C:]
    gbuf[halo:] = a * _sigmoid(gate)

    first = halo - (width - 1)
    rb, lb = CONV_ROW_BLOCK, CONV_LANE_BLOCK
    for r0 in range(0, tm, rb):
        for c0 in range(0, C, lb):
            lanes = slice(c0, c0 + lb)
            acc = jnp.zeros((rb, lb), F32) + dwb_ref[:, lanes]
            for j in range(width):
                acc = acc + dw_ref[j:j + 1, lanes] * gbuf[r0 + first + j:r0 + first + j + rb, lanes]
            ybuf[r0:r0 + rb, lanes] = acc
    gbuf[0:halo] = gbuf[tm:tm + halo]

    y = ybuf[...]
    mu = jnp.mean(y, axis=-1, keepdims=True)
    yc = y - mu
    var = jnp.mean(yc * yc, axis=-1, keepdims=True)
    z = yc * lax.rsqrt(var + NORM_EPS) * lng_ref[...] + lnb_ref[...]
    z = (z * _sigmoid(z)).astype(BF16)
    o_ref[0] = x + jnp.dot(z, wout_ref[...], preferred_element_type=F32) + bout_ref[...]


def _conv_mixer(x, gain, w_in, b_in, dw_w, dw_b, ln_g, ln_b, w_out, b_out):
    B, S, D = x.shape
    C = w_out.shape[0]
    width = dw_w.shape[0]
    assert width - 1 <= CONV_HALO
    tile = pl.BlockSpec((1, ROW_TILE, D), lambda b, i: (b, i, 0))
    row = lambda v: v.reshape(1, -1)
    args = (x, row(gain), w_in.astype(BF16), row(b_in), dw_w, row(dw_b), row(ln_g), row(ln_b),
            w_out.astype(BF16), row(b_out))
    return pl.pallas_call(
        functools.partial(_conv_kernel, width=width),
        grid=(B, S // ROW_TILE),
        in_specs=[tile] + [_resident(a.shape) for a in args[1:]],
        out_specs=tile,
        out_shape=jax.ShapeDtypeStruct(x.shape, x.dtype),
        scratch_shapes=[pltpu.VMEM((ROW_TILE + CONV_HALO, C), F32), pltpu.VMEM((ROW_TILE, C), F32)],
        compiler_params=_params(2),
        name="conformer_conv",
    )(*args)


def _hgrn_kernel(x_ref, g_ref, win_ref, lbl_ref, ng_ref, wout_ref, o_ref,
                 qbuf, bbuf, kbuf, vbuf, obuf, state, *, layer):
    ts = x_ref.shape[1]
    KD = qbuf.shape[1]
    hd = HGRN_HEAD_DIM
    ck = HGRN_CHUNK

    @pl.when(pl.program_id(1) == 0)
    def _():
        state[...] = jnp.zeros_like(state)

    lg = lbl_ref[...]
    e = jnp.exp(lg - jnp.max(lg, axis=0, keepdims=True))
    sm = e / jnp.sum(e, axis=0, keepdims=True)
    lb = jnp.zeros((1, KD), F32)
    for l in range(1, layer + 1):
        lb = lb + sm[l:l + 1]

    x = x_ref[0]
    hb = _rms_norm(x, g_ref[...]).astype(BF16)
    proj = lambda p: jnp.dot(hb, win_ref[:, p * KD:(p + 1) * KD], preferred_element_type=F32)
    q = proj(0)
    qbuf[...] = q * _sigmoid(q)
    fz = proj(1)
    logf = jnp.log(lb + (1.0 - lb) * _sigmoid(fz))
    kbuf[...] = (1.0 - lb) * _sigmoid(-fz)
    vbuf[...] = proj(2)
    pos = lax.broadcasted_iota(jnp.int32, (ts, 1), 0) % ck
    sh = 1
    while sh < ck:
        logf = logf + jnp.where(pos >= sh, pltpu.roll(logf, sh, 0), 0.0)
        sh *= 2
    bbuf[...] = logf

    row = lax.broadcasted_iota(jnp.int32, (ck, 1), 0)
    nt = (((1,), (1,)), ((), ()))
    tn = (((0,), (0,)), ((), ()))

    def chunk(c, carry):
        r0 = pl.multiple_of(c * ck, ck)
        rows = pl.ds(r0, ck)
        for h in range(HGRN_HEADS):
            lanes = slice(h * hd, (h + 1) * hd)
            bc = bbuf[rows, lanes]
            qc = qbuf[rows, lanes]
            kc = kbuf[rows, lanes]
            vc = vbuf[rows, lanes]
            st = state[h]
            o = lax.dot_general((qc * jnp.exp(bc)).astype(BF16), st.astype(BF16), nt,
                                preferred_element_type=F32)
            for s in range(ck):
                dec = jnp.where(row >= s, jnp.exp(bc - bc[s:s + 1]), 0.0)
                sc = jnp.sum(qc * dec * kc[s:s + 1], axis=-1, keepdims=True)
                o = o + sc * vc[s:s + 1]
            obuf[rows, lanes] = o
            bl = bc[ck - 1:ck]
            kd = (kc * jnp.exp(bl - bc)).astype(BF16)
            state[h] = st * jnp.exp(bl) + lax.dot_general(vc.astype(BF16), kd, tn,
                                                          preferred_element_type=F32)
        return carry

    lax.fori_loop(0, ts // ck, chunk, 0)

    gz = proj(3)
    gz = gz * _sigmoid(gz)
    for h in range(HGRN_HEADS):
        lanes = slice(h * hd, (h + 1) * hd)
        obuf[:, lanes] = _rms_norm(obuf[:, lanes], ng_ref[:, lanes]) * gz[:, lanes]
    o_ref[0] = x + jnp.dot(obuf[...].astype(BF16), wout_ref[...], preferred_element_type=F32)


def _hgrn_mixer(x, gain, w_in, lb_logits, norm_g, w_out, layer):
    B, S, D = x.shape
    KD = HGRN_HEADS * HGRN_HEAD_DIM
    assert w_in.shape[1] == 4 * KD and w_out.shape[0] == KD and S % HGRN_ROWS == 0
    tile = pl.BlockSpec((1, HGRN_ROWS, D), lambda b, i: (b, i, 0))
    buf = pltpu.VMEM((HGRN_ROWS, KD), F32)
    return pl.pallas_call(
        functools.partial(_hgrn_kernel, layer=layer),
        grid=(B, S // HGRN_ROWS),
        in_specs=[tile, _resident((1, D)), _resident(w_in.shape), _resident(lb_logits.shape),
                  _resident((1, KD)), _resident(w_out.shape)],
        out_specs=tile,
        out_shape=jax.ShapeDtypeStruct(x.shape, x.dtype),
        scratch_shapes=[buf, buf, buf, buf, buf,
                        pltpu.VMEM((HGRN_HEADS, HGRN_HEAD_DIM, HGRN_HEAD_DIM), F32)],
        compiler_params=_params(2),
        name="hgrn2",
    )(x, gain.reshape(1, D), w_in.astype(BF16), lb_logits.astype(F32), norm_g.reshape(1, KD),
      w_out.astype(BF16))


def kernel(x, mixer_norm, ffn_norm, attn_w_in, attn_q_gain, attn_k_gain, attn_w_out, conv_w_in, conv_b_in, conv_dw_w, conv_dw_b, conv_ln_g, conv_ln_b, conv_w_out, conv_b_out, hgrn_w_in, hgrn_lb_logits, hgrn_norm_g, hgrn_w_out, ffn_w_up, ffn_conv_w, ffn_conv_b, ffn_w_down):
    depth = mixer_norm.shape[0]
    for layer in range(depth):
        kind = layer % N_MIXERS
        j = layer // N_MIXERS
        if kind == 0:
            x = _attention_mixer(x, mixer_norm[layer], attn_w_in[j], attn_q_gain[j], attn_k_gain[j],
                                 attn_w_out[j])
        elif kind == 1:
            x = _conv_mixer(x, mixer_norm[layer], conv_w_in[j], conv_b_in[j], conv_dw_w[j], conv_dw_b[j],
                            conv_ln_g[j], conv_ln_b[j], conv_w_out[j], conv_b_out[j])
        else:
            x = _hgrn_mixer(x, mixer_norm[layer], hgrn_w_in[j], hgrn_lb_logits, hgrn_norm_g[j],
                            hgrn_w_out[j], layer)
        x = _ffn(x, ffn_norm[layer], ffn_w_up[layer], ffn_conv_w[layer], ffn_conv_b[layer],
                 ffn_w_down[layer])
    return x
```

```python
import functools

import jax
import jax.numpy as jnp
from jax import lax
from jax.experimental import pallas as pl
from jax.experimental.pallas import tpu as pltpu

F32 = jnp.float32
BF16 = jnp.bfloat16

NORM_EPS = 1e-6
N_MIXERS = 3
LANES = 128

ATTN_GROUPS = ((128, 1), (512, 4), (2048, 16))
ATTN_HEADS = 8
ATTN_HEAD_DIM = 64
ATTN_BLOCK = 128
ATTN_WIDTH = ATTN_HEADS * ATTN_HEAD_DIM
ATTN_PARTS = 3 * len(ATTN_GROUPS)
ATTN_Q_ROWS = 512
LSE_LANES = 16

HGRN_HEADS = 8
HGRN_HEAD_DIM = 128
HGRN_CHUNK = 16
HGRN_ROWS = 512

CONV_ROW_BLOCK = 64
CONV_LANE_BLOCK = 128
CONV_HALO = 32

FFN_CHUNK = 256
FFN_HALO = 8
ROW_TILE = 512

VMEM_LIMIT = 56 * 1024 * 1024


def _params(n_axes):
    return pltpu.CompilerParams(dimension_semantics=("arbitrary",) * n_axes,
                                vmem_limit_bytes=VMEM_LIMIT)


def _resident(shape):
    zeros = (0,) * len(shape)
    return pl.BlockSpec(shape, lambda *_: zeros, pipeline_mode=pl.Buffered(1))


def _rms_norm(x, gain):
    ms = jnp.mean(x * x, axis=-1, keepdims=True)
    return x * lax.rsqrt(ms + NORM_EPS) * gain


def _sigmoid(x):
    return 1.0 / (1.0 + jnp.exp(-x))


def _shift_rows(u, prev, k):
    r = pltpu.roll(u, k, 0)
    p = pltpu.roll(prev, k, 0)
    row = lax.broadcasted_iota(jnp.int32, prev.shape, 0)
    head = jnp.where(row < k, p, r[:8])
    return jnp.concatenate([head, r[8:]], axis=0)


def _ffn_kernel(x_ref, g_ref, wup_ref, cw_ref, cb_ref, wdn_ref, o_ref,
                halo_ref, acc_ref, hb_ref, ubuf, abuf, *, nc, ch):
    @pl.when(pl.program_id(1) == 0)
    def _():
        halo_ref[...] = jnp.zeros_like(halo_ref)

    x = x_ref[0]
    tm = x.shape[0]
    hb_ref[...] = _rms_norm(x, g_ref[...]).astype(BF16)
    acc_ref[...] = x

    def down(c):
        acc_ref[...] += jnp.dot(abuf[...], wdn_ref[c], preferred_element_type=F32)

    def act(c):
        ubuf[0:FFN_HALO] = halo_ref[c]
        cw = cw_ref[c]
        y = (ubuf[FFN_HALO:] * cw[2:3] + ubuf[FFN_HALO - 1:tm + FFN_HALO - 1] * cw[1:2]
             + ubuf[FFN_HALO - 2:tm + FFN_HALO - 2] * cw[0:1] + cb_ref[c])
        halo_ref[c] = ubuf[tm:]
        gate = y[:, :ch]
        abuf[...] = (gate * _sigmoid(gate) * y[:, ch:]).astype(BF16)

    def up(c):
        ubuf[FFN_HALO:] = jnp.dot(hb_ref[...], wup_ref[c], preferred_element_type=F32)

    up(0)
    act(0)
    up(1)

    def step(c, carry):
        down(c - 1)
        act(c)
        up(c + 1)
        return carry

    lax.fori_loop(1, nc - 1, step, 0)
    down(nc - 2)
    act(nc - 1)
    down(nc - 1)
    o_ref[0] = acc_ref[...]


def _ffn(x, gain, w_up, conv_w, conv_b, w_down):
    B, S, D = x.shape
    F = w_down.shape[0]
    ch = FFN_CHUNK
    nc = F // ch
    assert nc * ch == F and nc >= 2 and S % ROW_TILE == 0
    wup = w_up.astype(BF16).reshape(D, 2, nc, ch).transpose(2, 0, 1, 3).reshape(nc, D, 2 * ch)
    cw = conv_w.reshape(3, 2, nc, ch).transpose(2, 0, 1, 3).reshape(nc, 3, 2 * ch)
    cb = conv_b.reshape(2, nc, ch).transpose(1, 0, 2).reshape(nc, 1, 2 * ch)
    wdn = w_down.astype(BF16).reshape(nc, ch, D)
    tile = pl.BlockSpec((1, ROW_TILE, D), lambda b, i: (b, i, 0))
    return pl.pallas_call(
        functools.partial(_ffn_kernel, nc=nc, ch=ch),
        grid=(B, S // ROW_TILE),
        in_specs=[tile, _resident((1, D)), _resident(wup.shape), _resident(cw.shape),
                  _resident(cb.shape), _resident(wdn.shape)],
        out_specs=tile,
        out_shape=jax.ShapeDtypeStruct(x.shape, x.dtype),
        scratch_shapes=[pltpu.VMEM((nc, FFN_HALO, 2 * ch), F32), pltpu.VMEM((ROW_TILE, D), F32),
                        pltpu.VMEM((ROW_TILE, D), BF16),
                        pltpu.VMEM((ROW_TILE + FFN_HALO, 2 * ch), F32), pltpu.VMEM((ROW_TILE, ch), BF16)],
        compiler_params=_params(2),
        name="ffn",
    )(x, gain.reshape(1, D), wup, cw, cb, wdn)


def _attn_in_kernel(x_ref, g_ref, w_ref, qk_ref, bd_ref, o0_ref, o1_ref, o2_ref, ybuf):
    outs = (o0_ref, o1_ref, o2_ref)
    tm = x_ref.shape[1]
    hb = _rms_norm(x_ref[0], g_ref[...]).astype(BF16)
    bd = bd_ref[...]
    half = bd.shape[0]
    n_lane_tiles = ATTN_WIDTH // LANES
    for j in range(ATTN_PARTS):
        g, part = divmod(j, 3)
        d = ATTN_GROUPS[g][1]
        acc = jnp.dot(hb, w_ref[:, j * ATTN_WIDTH:(j + 1) * ATTN_WIDTH], preferred_element_type=F32)
        if part != 2:
            sq = (acc * acc).astype(BF16)
            ssq = jnp.concatenate(
                [jnp.dot(sq[:, :half], bd, preferred_element_type=F32),
                 jnp.dot(sq[:, half:], bd, preferred_element_type=F32)], axis=1)
            acc = acc * lax.rsqrt(ssq * (1.0 / ATTN_HEAD_DIM) + NORM_EPS) * qk_ref[j:j + 1, :]
        cols = slice(part * ATTN_WIDTH, (part + 1) * ATTN_WIDTH)
        if d == 1:
            outs[g][0, 0, :, cols] = acc.astype(BF16)
        else:
            for c in range(n_lane_tiles):
                ybuf[c] = acc[:, c * LANES:(c + 1) * LANES]
            for r in range(d):
                piece = [ybuf[c, pl.ds(r, tm // d, stride=d), :] for c in range(n_lane_tiles)]
                outs[g][0, r, :, cols] = jnp.concatenate(piece, axis=1).astype(BF16)


def _attn_in(x, gain, w_in, q_gain, k_gain):
    B, S, D = x.shape
    W = w_in.shape[1]
    scale = ATTN_HEAD_DIM ** -0.5
    rows = []
    for g in range(len(ATTN_GROUPS)):
        rows += [jnp.tile(q_gain[g], ATTN_HEADS) * scale, jnp.tile(k_gain[g], ATTN_HEADS),
                 jnp.ones((ATTN_WIDTH,), F32)]
    qk = jnp.stack(rows).astype(F32)
    half = ATTN_WIDTH // 2
    head = jnp.arange(half) // ATTN_HEAD_DIM
    bd = (head[:, None] == head[None, :]).astype(BF16)
    out_specs, out_shape = [], []
    for _, d in ATTN_GROUPS:
        assert ROW_TILE % (16 * d) == 0
        out_specs.append(pl.BlockSpec((1, d, ROW_TILE // d, 3 * ATTN_WIDTH), lambda b, i: (b, 0, i, 0)))
        out_shape.append(jax.ShapeDtypeStruct((B, d, S // d, 3 * ATTN_WIDTH), BF16))
    return pl.pallas_call(
        _attn_in_kernel,
        grid=(B, S // ROW_TILE),
        in_specs=[pl.BlockSpec((1, ROW_TILE, D), lambda b, i: (b, i, 0)), _resident((1, D)),
                  _resident((D, W)), _resident(qk.shape), _resident(bd.shape)],
        out_specs=out_specs,
        out_shape=out_shape,
        scratch_shapes=[pltpu.VMEM((ATTN_WIDTH // LANES, ROW_TILE, LANES), F32)],
        compiler_params=_params(2),
        name="attn_in",
    )(x, gain.reshape(1, D), w_in.astype(BF16), qk, bd)


def _attn_core_kernel(q_ref, k_ref, v_ref, bias_ref, o_ref, lse_ref, kbuf, vbuf):
    blk = ATTN_BLOCK
    nq = q_ref.shape[2]
    first = pl.program_id(2) == 0

    @pl.when(first)
    def _():
        kbuf[0:blk] = jnp.zeros((blk, ATTN_WIDTH), BF16)
        vbuf[0:blk] = jnp.zeros((blk, ATTN_WIDTH), BF16)

    kbuf[blk:] = k_ref[0, 0]
    vbuf[blk:] = v_ref[0, 0]
    lo = lax.broadcasted_iota(jnp.int32, (1, 2 * ATTN_HEAD_DIM), 1) < ATTN_HEAD_DIM
    lse_head = lax.broadcasted_iota(jnp.int32, (1, ATTN_HEADS * LSE_LANES), 1) // LSE_LANES
    zero = jnp.zeros((), BF16)
    nt = (((1,), (1,)), ((), ()))
    for jq in range(nq // blk):
        bias = bias_ref[1]
        if jq == 0:
            bias = jnp.where(first, bias_ref[0], bias)
        rows = slice(jq * blk, (jq + 1) * blk)
        lse = jnp.zeros((blk, ATTN_HEADS * LSE_LANES), F32)
        for hp in range(ATTN_HEADS // 2):
            lanes = slice(hp * 2 * ATTN_HEAD_DIM, (hp + 1) * 2 * ATTN_HEAD_DIM)
            q2 = q_ref[0, 0, rows, lanes]
            kk = kbuf[jq * blk:(jq + 2) * blk, lanes]
            vv = vbuf[jq * blk:(jq + 2) * blk, lanes]
            qq = jnp.concatenate([jnp.where(lo, q2, zero), jnp.where(lo, zero, q2)], axis=0)
            s2 = lax.dot_general(qq, kk, nt, preferred_element_type=F32)
            ps, invs = [], []
            for e in range(2):
                s = s2[e * blk:(e + 1) * blk] + bias
                m = jnp.max(s, axis=-1, keepdims=True)
                p = jnp.exp(s - m)
                l = jnp.sum(p, axis=-1, keepdims=True)
                ps.append(p.astype(BF16))
                invs.append(1.0 / l)
                lse = jnp.where(lse_head == 2 * hp + e, m + jnp.log(l), lse)
            pcat = jnp.concatenate(ps, axis=1)
            vcat = jnp.concatenate([jnp.where(lo, vv, zero), jnp.where(lo, zero, vv)], axis=0)
            o2 = jnp.dot(pcat, vcat, preferred_element_type=F32)
            o_ref[0, 0, rows, lanes] = (o2 * jnp.where(lo, invs[0], invs[1])).astype(o_ref.dtype)
        lse_ref[0, 0, rows, :] = lse
    kbuf[0:blk] = kbuf[nq:nq + blk]
    vbuf[0:blk] = vbuf[nq:nq + blk]


def _attn_core(qkv, window, dilation):
    B, d, n, _ = qkv.shape
    blk = ATTN_BLOCK
    assert d == dilation and window // d == blk and n % ATTN_Q_ROWS == 0
    qi = jnp.arange(blk)[:, None]
    kj = jnp.arange(2 * blk)[None, :]
    dist = qi + blk - kj
    band = (dist >= 0) & (dist <= blk)
    neg = jnp.float32(-jnp.inf)
    bias = jnp.stack([jnp.where(band & (kj >= blk), 0.0, neg), jnp.where(band, 0.0, neg)]).astype(F32)

    def spec(width, part):
        return pl.BlockSpec((1, 1, ATTN_Q_ROWS, width), lambda b, r, i: (b, r, i, part))

    lse_width = ATTN_HEADS * LSE_LANES
    return pl.pallas_call(
        _attn_core_kernel,
        grid=(B, d, n // ATTN_Q_ROWS),
        in_specs=[spec(ATTN_WIDTH, 0), spec(ATTN_WIDTH, 1), spec(ATTN_WIDTH, 2), _resident(bias.shape)],
        out_specs=[spec(ATTN_WIDTH, 0), spec(lse_width, 0)],
        out_shape=[jax.ShapeDtypeStruct((B, d, n, ATTN_WIDTH), BF16),
                   jax.ShapeDtypeStruct((B, d, n, lse_width), F32)],
        scratch_shapes=[pltpu.VMEM((ATTN_Q_ROWS + blk, ATTN_WIDTH), BF16),
                        pltpu.VMEM((ATTN_Q_ROWS + blk, ATTN_WIDTH), BF16)],
        compiler_params=_params(3),
        name=f"attn_core_d{d}",
    )(qkv, qkv, qkv, bias)


def _attn_out_kernel(x_ref, o0, o1, o2, l0, l1, l2, w_ref, ex_ref, out_ref, ob1, ob2, lb1, lb2):
    tm = x_ref.shape[1]
    outs = [o0[0, 0].astype(F32)]
    lses = [l0[0, 0]]
    for o_ref, l_ref, ob, lb in ((o1, l1, ob1, lb1), (o2, l2, ob2, lb2)):
        d = o_ref.shape[1]
        n_lane_tiles = ob.shape[0]
        for r in range(d):
            o_r = o_ref[0, r].astype(F32)
            for c in range(n_lane_tiles):
                ob[c, pl.ds(r, tm // d, stride=d), :] = o_r[:, c * LANES:(c + 1) * LANES]
            lb[pl.ds(r, tm // d, stride=d), :] = l_ref[0, r]
        outs.append(jnp.concatenate([ob[c] for c in range(n_lane_tiles)], axis=1))
        lses.append(lb[...])
    m = jnp.maximum(jnp.maximum(lses[0], lses[1]), lses[2])
    es = [jnp.exp(l - m) for l in lses]
    inv = 1.0 / (es[0] + es[1] + es[2])
    ex = ex_ref[...]
    mix = None
    for e, o in zip(es, outs):
        w = e * inv
        hi = w.astype(BF16)
        lo = (w - hi.astype(F32)).astype(BF16)
        wide = jnp.dot(hi, ex, preferred_element_type=F32) + jnp.dot(lo, ex, preferred_element_type=F32)
        mix = wide * o if mix is None else mix + wide * o
    out_ref[0] = x_ref[0] + jnp.dot(mix.astype(BF16), w_ref[...], preferred_element_type=F32)


def _attn_out(x, outs, lses, w_out):
    B, S, D = x.shape
    lse_width = ATTN_HEADS * LSE_LANES
    tile = pl.BlockSpec((1, ROW_TILE, D), lambda b, i: (b, i, 0))

    def part(width, d):
        return pl.BlockSpec((1, d, ROW_TILE // d, width), lambda b, i: (b, 0, i, 0))

    dil = [d for _, d in ATTN_GROUPS]
    assert dil[0] == 1
    lane = jnp.arange(lse_width)[:, None]
    col = jnp.arange(ATTN_WIDTH)[None, :]
    expand = (lane == (col // ATTN_HEAD_DIM) * LSE_LANES).astype(BF16)
    return pl.pallas_call(
        _attn_out_kernel,
        grid=(B, S // ROW_TILE),
        in_specs=[tile] + [part(ATTN_WIDTH, d) for d in dil] + [part(lse_width, d) for d in dil]
        + [_resident(w_out.shape), _resident(expand.shape)],
        out_specs=tile,
        out_shape=jax.ShapeDtypeStruct(x.shape, x.dtype),
        scratch_shapes=[pltpu.VMEM((ATTN_WIDTH // LANES, ROW_TILE, LANES), F32),
                        pltpu.VMEM((ATTN_WIDTH // LANES, ROW_TILE, LANES), F32),
                        pltpu.VMEM((ROW_TILE, lse_width), F32), pltpu.VMEM((ROW_TILE, lse_width), F32)],
        compiler_params=_params(2),
        name="attn_out",
    )(x, *outs, *lses, w_out.astype(BF16), expand)


def _attention_mixer(x, gain, w_in, q_gain, k_gain, w_out):
    qkvs = _attn_in(x, gain, w_in, q_gain, k_gain)
    outs, lses = [], []
    for qkv, (window, dilation) in zip(qkvs, ATTN_GROUPS):
        o, lse = _attn_core(qkv, window, dilation)
        outs.append(o)
        lses.append(lse)
    return _attn_out(x, outs, lses, w_out)


def _conv_kernel(x_ref, g_ref, win_ref, bin_ref, dw_ref, dwb_ref, lng_ref, lnb_ref, wout_ref, bout_ref,
                 o_ref, gbuf, ybuf, *, width):
    halo = CONV_HALO
    tm = x_ref.shape[1]
    C = gbuf.shape[1]

    @pl.when(pl.program_id(1) == 0)
    def _():
        gbuf[0:halo] = jnp.zeros((halo, C), F32)

    x = x_ref[0]
    hb = _rms_norm(x, g_ref[...]).astype(BF16)
    a = jnp.dot(hb, win_ref[:, :C], preferred_element_type=F32) + bin_ref[:, :C]
    gate = jnp.dot(hb, win_ref[:, C:], preferred_element_type=F32) + bin_ref[:, C:]
    gbuf[halo:] = a * _sigmoid(gate)

    first = halo - (width - 1)
    rb, lb = CONV_ROW_BLOCK, CONV_LANE_BLOCK
    nwin = rb + halo

    def row_block(i, carry):
        r0 = pl.multiple_of(i * rb, rb)
        for c0 in range(0, C, lb):
            lanes = slice(c0, c0 + lb)
            win = gbuf[pl.ds(r0, nwin), lanes]
            acc = jnp.zeros((rb, lb), F32) + dwb_ref[:, lanes]
            for phase in range(8):
                taps = [j for j in range(width) if (first + j) % 8 == phase]
                if not taps:
                    continue
                shifted = win if phase == 0 else pltpu.roll(win, nwin - phase, 0)
                for j in taps:
                    a8 = (first + j) // 8 * 8
                    acc = acc + dw_ref[j:j + 1, lanes] * shifted[a8:a8 + rb]
            ybuf[pl.ds(r0, rb), lanes] = acc
        return carry

    lax.fori_loop(0, tm // rb, row_block, 0)
    gbuf[0:halo] = gbuf[tm:tm + halo]

    y = ybuf[...]
    mu = jnp.mean(y, axis=-1, keepdims=True)
    yc = y - mu
    var = jnp.mean(yc * yc, axis=-1, keepdims=True)
    z = yc * lax.rsqrt(var + NORM_EPS) * lng_ref[...] + lnb_ref[...]
    z = (z * _sigmoid(z)).astype(BF16)
    o_ref[0] = x + jnp.dot(z, wout_ref[...], preferred_element_type=F32) + bout_ref[...]


def _conv_mixer(x, gain, w_in, b_in, dw_w, dw_b, ln_g, ln_b, w_out, b_out):
    B, S, D = x.shape
    C = w_out.shape[0]
    width = dw_w.shape[0]
    assert width - 1 <= CONV_HALO
    tile = pl.BlockSpec((1, ROW_TILE, D), lambda b, i: (b, i, 0))
    row = lambda v: v.reshape(1, -1)
    args = (x, row(gain), w_in.astype(BF16), row(b_in), dw_w, row(dw_b), row(ln_g), row(ln_b),
            w_out.astype(BF16), row(b_out))
    return pl.pallas_call(
        functools.partial(_conv_kernel, width=width),
        grid=(B, S // ROW_TILE),
        in_specs=[tile] + [_resident(a.shape) for a in args[1:]],
        out_specs=tile,
        out_shape=jax.ShapeDtypeStruct(x.shape, x.dtype),
        scratch_shapes=[pltpu.VMEM((ROW_TILE + CONV_HALO, C), F32), pltpu.VMEM((ROW_TILE, C), F32)],
        compiler_params=_params(2),
        name="conformer_conv",
    )(*args)


def _hgrn_kernel(x_ref, g_ref, win_ref, lbl_ref, ng_ref, wout_ref, o_ref,
                 qbuf, bbuf, kbuf, vbuf, obuf, state, *, layer):
    ts = x_ref.shape[1]
    KD = qbuf.shape[1]
    hd = HGRN_HEAD_DIM
    ck = HGRN_CHUNK

    @pl.when(pl.program_id(1) == 0)
    def _():
        state[...] = jnp.zeros_like(state)

    lg = lbl_ref[...]
    e = jnp.exp(lg - jnp.max(lg, axis=0, keepdims=True))
    sm = e / jnp.sum(e, axis=0, keepdims=True)
    lb = jnp.zeros((1, KD), F32)
    for l in range(1, layer + 1):
        lb = lb + sm[l:l + 1]

    x = x_ref[0]
    hb = _rms_norm(x, g_ref[...]).astype(BF16)
    proj = lambda p: jnp.dot(hb, win_ref[:, p * KD:(p + 1) * KD], preferred_element_type=F32)
    q = proj(0)
    qbuf[...] = q * _sigmoid(q)
    fz = proj(1)
    logf = jnp.log(lb + (1.0 - lb) * _sigmoid(fz))
    kbuf[...] = (1.0 - lb) * _sigmoid(-fz)
    vbuf[...] = proj(2)
    pos = lax.broadcasted_iota(jnp.int32, (ts, 1), 0) % ck
    sh = 1
    while sh < ck:
        logf = logf + jnp.where(pos >= sh, pltpu.roll(logf, sh, 0), 0.0)
        sh *= 2
    bbuf[...] = logf

    row8 = lax.broadcasted_iota(jnp.int32, (8, 1), 0)
    nt = (((1,), (1,)), ((), ()))
    tn = (((0,), (0,)), ((), ()))

    def chunk(c, carry):
        r0 = pl.multiple_of(c * ck, ck)
        rows = pl.ds(r0, ck)
        for h in range(HGRN_HEADS):
            lanes = slice(h * hd, (h + 1) * hd)
            bc = bbuf[rows, lanes]
            qc = qbuf[rows, lanes]
            kc = kbuf[rows, lanes]
            vc = vbuf[rows, lanes]
            st = state[h]
            o = lax.dot_general((qc * jnp.exp(bc)).astype(BF16), st.astype(BF16), nt,
                                preferred_element_type=F32)
            halves = []
            for t0 in range(0, ck, 8):
                bh, qh, oh = bc[t0:t0 + 8], qc[t0:t0 + 8], o[t0:t0 + 8]
                for s in range(min(ck, t0 + 8)):
                    dec = jnp.exp(bh - bc[s:s + 1])
                    if s > t0:
                        dec = jnp.where(row8 >= s - t0, dec, 0.0)
                    sc = jnp.sum(qh * dec * kc[s:s + 1], axis=-1, keepdims=True)
                    oh = oh + sc * vc[s:s + 1]
                halves.append(oh)
            obuf[rows, lanes] = jnp.concatenate(halves, axis=0)
            bl = bc[ck - 1:ck]
            kd = (kc * jnp.exp(bl - bc)).astype(BF16)
            state[h] = st * jnp.exp(bl) + lax.dot_general(vc.astype(BF16), kd, tn,
                                                          preferred_element_type=F32)
        return carry

    lax.fori_loop(0, ts // ck, chunk, 0)

    gz = proj(3)
    gz = gz * _sigmoid(gz)
    for h in range(HGRN_HEADS):
        lanes = slice(h * hd, (h + 1) * hd)
        obuf[:, lanes] = _rms_norm(obuf[:, lanes], ng_ref[:, lanes]) * gz[:, lanes]
    o_ref[0] = x + jnp.dot(obuf[...].astype(BF16), wout_ref[...], preferred_element_type=F32)


def _hgrn_mixer(x, gain, w_in, lb_logits, norm_g, w_out, layer):
    B, S, D = x.shape
    KD = HGRN_HEADS * HGRN_HEAD_DIM
    assert w_in.shape[1] == 4 * KD and w_out.shape[0] == KD and S % HGRN_ROWS == 0
    tile = pl.BlockSpec((1, HGRN_ROWS, D), lambda b, i: (b, i, 0))
    buf = pltpu.VMEM((HGRN_ROWS, KD), F32)
    return pl.pallas_call(
        functools.partial(_hgrn_kernel, layer=layer),
        grid=(B, S // HGRN_ROWS),
        in_specs=[tile, _resident((1, D)), _resident(w_in.shape), _resident(lb_logits.shape),
                  _resident((1, KD)), _resident(w_out.shape)],
        out_specs=tile,
        out_shape=jax.ShapeDtypeStruct(x.shape, x.dtype),
        scratch_shapes=[buf, buf, buf, buf, buf,
                        pltpu.VMEM((HGRN_HEADS, HGRN_HEAD_DIM, HGRN_HEAD_DIM), F32)],
        compiler_params=_params(2),
        name="hgrn2",
    )(x, gain.reshape(1, D), w_in.astype(BF16), lb_logits.astype(F32), norm_g.reshape(1, KD),
      w_out.astype(BF16))


def kernel(x, mixer_norm, ffn_norm, attn_w_in, attn_q_gain, attn_k_gain, attn_w_out, conv_w_in, conv_b_in, conv_dw_w, conv_dw_b, conv_ln_g, conv_ln_b, conv_w_out, conv_b_out, hgrn_w_in, hgrn_lb_logits, hgrn_norm_g, hgrn_w_out, ffn_w_up, ffn_conv_w, ffn_conv_b, ffn_w_down):
    depth = mixer_norm.shape[0]
    for layer in range(depth):
        kind = layer % N_MIXERS
        j = layer // N_MIXERS
        if kind == 0:
            x = _attention_mixer(x, mixer_norm[layer], attn_w_in[j], attn_q_gain[j], attn_k_gain[j],
                                 attn_w_out[j])
        elif kind == 1:
            x = _conv_mixer(x, mixer_norm[layer], conv_w_in[j], conv_b_in[j], conv_dw_w[j], conv_dw_b[j],
                            conv_ln_g[j], conv_ln_b[j], conv_w_out[j], conv_b_out[j])
        else:
            x = _hgrn_mixer(x, mixer_norm[layer], hgrn_w_in[j], hgrn_lb_logits, hgrn_norm_g[j],
                            hgrn_w_out[j], layer)
        x = _ffn(x, ffn_norm[layer], ffn_w_up[layer], ffn_conv_w[layer], ffn_conv_b[layer],
                 ffn_w_down[layer])
    return x
```

```python
import functools

import jax
import jax.numpy as jnp
from jax import lax
from jax.experimental import pallas as pl
from jax.experimental.pallas import tpu as pltpu

F32 = jnp.float32
BF16 = jnp.bfloat16

NORM_EPS = 1e-6
N_MIXERS = 3
LANES = 128
LN2 = 0.6931471805599453
LOG2E = 1.4426950408889634

ATTN_GROUPS = ((128, 1), (512, 4), (2048, 16))
ATTN_HEADS = 8
ATTN_HEAD_DIM = 64
ATTN_BLOCK = 128
ATTN_WIDTH = ATTN_HEADS * ATTN_HEAD_DIM
ATTN_PARTS = 3 * len(ATTN_GROUPS)
ATTN_Q_ROWS = 512
LSE_LANES = 16

HGRN_HEADS = 8
HGRN_HEAD_DIM = 128
HGRN_CHUNK = 16
HGRN_ROWS = 512

CONV_ROW_BLOCK = 64
CONV_LANE_BLOCK = 128
CONV_HALO = 32

FFN_CHUNK = 256
FFN_HALO = 8
FFN_ROWS = 512
ROW_TILE = 512

VMEM_LIMIT = 56 * 1024 * 1024


def _params(n_axes):
    return pltpu.CompilerParams(dimension_semantics=("arbitrary",) * n_axes,
                                vmem_limit_bytes=VMEM_LIMIT)


def _resident(shape):
    zeros = (0,) * len(shape)
    return pl.BlockSpec(shape, lambda *_: zeros, pipeline_mode=pl.Buffered(1))


def _rms_norm(x, gain):
    ms = jnp.mean(x * x, axis=-1, keepdims=True)
    return x * lax.rsqrt(ms + NORM_EPS) * gain


def _sigmoid(x):
    return 1.0 / (1.0 + jnp.exp(-x))


def _shift_rows(u, prev, k):
    r = pltpu.roll(u, k, 0)
    p = pltpu.roll(prev, k, 0)
    row = lax.broadcasted_iota(jnp.int32, prev.shape, 0)
    head = jnp.where(row < k, p, r[:8])
    return jnp.concatenate([head, r[8:]], axis=0)


def _ffn_body(x, g_ref, wup_ref, cw_ref, cb_ref, wdn_ref, o_ref,
              halo_ref, acc_ref, hb_ref, ubuf, abuf, *, nc, ch):
    @pl.when(pl.program_id(1) == 0)
    def _():
        halo_ref[...] = jnp.zeros_like(halo_ref)

    tm = x.shape[0]
    hb_ref[...] = _rms_norm(x, g_ref[...]).astype(BF16)
    acc_ref[...] = x

    def down(c):
        acc_ref[...] += jnp.dot(abuf[...], wdn_ref[c], preferred_element_type=F32)

    def act(c):
        ubuf[0:FFN_HALO] = halo_ref[c]
        cw = cw_ref[c]
        y = (ubuf[FFN_HALO:] * cw[2:3] + ubuf[FFN_HALO - 1:tm + FFN_HALO - 1] * cw[1:2]
             + ubuf[FFN_HALO - 2:tm + FFN_HALO - 2] * cw[0:1] + cb_ref[c])
        halo_ref[c] = ubuf[tm:]
        gate = y[:, :ch]
        abuf[...] = (gate * _sigmoid(gate) * y[:, ch:]).astype(BF16)

    def up(c):
        ubuf[FFN_HALO:] = jnp.dot(hb_ref[...], wup_ref[c], preferred_element_type=F32)

    up(0)
    act(0)
    up(1)
    for c in range(1, nc - 1):
        down(c - 1)
        act(c)
        up(c + 1)
    down(nc - 2)
    act(nc - 1)
    down(nc - 1)
    o_ref[0] = acc_ref[...]


def _ffn_kernel(x_ref, *refs, nc, ch):
    _ffn_body(x_ref[0], *refs, nc=nc, ch=ch)


def _ffn_operands(gain, w_up, conv_w, conv_b, w_down):
    D = w_up.shape[0]
    F = w_down.shape[0]
    ch = FFN_CHUNK
    nc = F // ch
    assert nc * ch == F and nc >= 2
    wup = w_up.astype(BF16).reshape(D, 2, nc, ch).transpose(2, 0, 1, 3).reshape(nc, D, 2 * ch)
    cw = conv_w.reshape(3, 2, nc, ch).transpose(2, 0, 1, 3).reshape(nc, 3, 2 * ch)
    cb = conv_b.reshape(2, nc, ch).transpose(1, 0, 2).reshape(nc, 1, 2 * ch)
    wdn = w_down.astype(BF16).reshape(nc, ch, D)
    args = (gain.reshape(1, D), wup, cw, cb, wdn)
    scratch = [pltpu.VMEM((nc, FFN_HALO, 2 * ch), F32), pltpu.VMEM((FFN_ROWS, D), F32),
               pltpu.VMEM((FFN_ROWS, D), BF16),
               pltpu.VMEM((FFN_ROWS + FFN_HALO, 2 * ch), F32), pltpu.VMEM((FFN_ROWS, ch), BF16)]
    return args, [_resident(a.shape) for a in args], scratch, dict(nc=nc, ch=ch)


def _ffn(x, gain, w_up, conv_w, conv_b, w_down):
    B, S, D = x.shape
    assert S % FFN_ROWS == 0
    args, specs, scratch, static = _ffn_operands(gain, w_up, conv_w, conv_b, w_down)
    tile = pl.BlockSpec((1, FFN_ROWS, D), lambda b, i: (b, i, 0))
    return pl.pallas_call(
        functools.partial(_ffn_kernel, **static),
        grid=(B, S // FFN_ROWS),
        in_specs=[tile] + specs,
        out_specs=tile,
        out_shape=jax.ShapeDtypeStruct(x.shape, x.dtype),
        scratch_shapes=scratch,
        compiler_params=_params(2),
        name="ffn",
    )(x, *args)


def _attn_in_kernel(x_ref, g_ref, w_ref, qk_ref, bd_ref, o0_ref, o1_ref, o2_ref, ybuf):
    outs = (o0_ref, o1_ref, o2_ref)
    tm = x_ref.shape[1]
    hb = _rms_norm(x_ref[0], g_ref[...]).astype(BF16)
    bd = bd_ref[...]
    half = bd.shape[0]
    n_lane_tiles = ATTN_WIDTH // LANES
    for j in range(ATTN_PARTS):
        g, part = divmod(j, 3)
        d = ATTN_GROUPS[g][1]
        acc = jnp.dot(hb, w_ref[:, j * ATTN_WIDTH:(j + 1) * ATTN_WIDTH], preferred_element_type=F32)
        if part != 2:
            sq = (acc * acc).astype(BF16)
            ssq = jnp.concatenate(
                [jnp.dot(sq[:, :half], bd, preferred_element_type=F32),
                 jnp.dot(sq[:, half:], bd, preferred_element_type=F32)], axis=1)
            acc = acc * lax.rsqrt(ssq * (1.0 / ATTN_HEAD_DIM) + NORM_EPS) * qk_ref[j:j + 1, :]
        cols = slice(part * ATTN_WIDTH, (part + 1) * ATTN_WIDTH)
        if d == 1:
            outs[g][0, 0, :, cols] = acc.astype(BF16)
        else:
            for c in range(n_lane_tiles):
                ybuf[c] = acc[:, c * LANES:(c + 1) * LANES]
            for r in range(d):
                piece = [ybuf[c, pl.ds(r, tm // d, stride=d), :] for c in range(n_lane_tiles)]
                outs[g][0, r, :, cols] = jnp.concatenate(piece, axis=1).astype(BF16)


def _attn_in(x, gain, w_in, q_gain, k_gain):
    B, S, D = x.shape
    W = w_in.shape[1]
    scale = ATTN_HEAD_DIM ** -0.5 * LOG2E
    rows = []
    for g in range(len(ATTN_GROUPS)):
        rows += [jnp.tile(q_gain[g], ATTN_HEADS) * scale, jnp.tile(k_gain[g], ATTN_HEADS),
                 jnp.ones((ATTN_WIDTH,), F32)]
    qk = jnp.stack(rows).astype(F32)
    half = ATTN_WIDTH // 2
    head = jnp.arange(half) // ATTN_HEAD_DIM
    bd = (head[:, None] == head[None, :]).astype(BF16)
    out_specs, out_shape = [], []
    for _, d in ATTN_GROUPS:
        assert ROW_TILE % (16 * d) == 0
        out_specs.append(pl.BlockSpec((1, d, ROW_TILE // d, 3 * ATTN_WIDTH), lambda b, i: (b, 0, i, 0)))
        out_shape.append(jax.ShapeDtypeStruct((B, d, S // d, 3 * ATTN_WIDTH), BF16))
    return pl.pallas_call(
        _attn_in_kernel,
        grid=(B, S // ROW_TILE),
        in_specs=[pl.BlockSpec((1, ROW_TILE, D), lambda b, i: (b, i, 0)), _resident((1, D)),
                  _resident((D, W)), _resident(qk.shape), _resident(bd.shape)],
        out_specs=out_specs,
        out_shape=out_shape,
        scratch_shapes=[pltpu.VMEM((ATTN_WIDTH // LANES, ROW_TILE, LANES), F32)],
        compiler_params=_params(2),
        name="attn_in",
    )(x, gain.reshape(1, D), w_in.astype(BF16), qk, bd)


def _attn_core_kernel(q_ref, k_ref, v_ref, bias_ref, o_ref, lse_ref, kbuf, vbuf):
    blk = ATTN_BLOCK
    nq = q_ref.shape[2]
    first = pl.program_id(2) == 0

    @pl.when(first)
    def _():
        kbuf[0:blk] = jnp.zeros((blk, ATTN_WIDTH), BF16)
        vbuf[0:blk] = jnp.zeros((blk, ATTN_WIDTH), BF16)

    kbuf[blk:] = k_ref[0, 0]
    vbuf[blk:] = v_ref[0, 0]
    lo = lax.broadcasted_iota(jnp.int32, (1, 2 * ATTN_HEAD_DIM), 1) < ATTN_HEAD_DIM
    lse_head = lax.broadcasted_iota(jnp.int32, (1, ATTN_HEADS * LSE_LANES), 1) // LSE_LANES
    zero = jnp.zeros((), BF16)
    nt = (((1,), (1,)), ((), ()))
    for jq in range(nq // blk):
        bias = bias_ref[1]
        if jq == 0:
            bias = jnp.where(first, bias_ref[0], bias)
        rows = slice(jq * blk, (jq + 1) * blk)
        lse = jnp.zeros((blk, ATTN_HEADS * LSE_LANES), F32)
        for hp in range(ATTN_HEADS // 2):
            lanes = slice(hp * 2 * ATTN_HEAD_DIM, (hp + 1) * 2 * ATTN_HEAD_DIM)
            q2 = q_ref[0, 0, rows, lanes]
            kk = kbuf[jq * blk:(jq + 2) * blk, lanes]
            vv = vbuf[jq * blk:(jq + 2) * blk, lanes]
            qq = jnp.concatenate([jnp.where(lo, q2, zero), jnp.where(lo, zero, q2)], axis=0)
            s2 = lax.dot_general(qq, kk, nt, preferred_element_type=F32)
            ps, invs = [], []
            for e in range(2):
                s = s2[e * blk:(e + 1) * blk] + bias
                m = jnp.max(s, axis=-1, keepdims=True)
                p = jnp.exp2(s - m)
                l = jnp.sum(p, axis=-1, keepdims=True)
                ps.append(p.astype(BF16))
                invs.append(1.0 / l)
                lse = jnp.where(lse_head == 2 * hp + e, m * LN2 + jnp.log(l), lse)
            pcat = jnp.concatenate(ps, axis=1)
            vcat = jnp.concatenate([jnp.where(lo, vv, zero), jnp.where(lo, zero, vv)], axis=0)
            o2 = jnp.dot(pcat, vcat, preferred_element_type=F32)
            o_ref[0, 0, rows, lanes] = (o2 * jnp.where(lo, invs[0], invs[1])).astype(o_ref.dtype)
        lse_ref[0, 0, rows, :] = lse
    kbuf[0:blk] = kbuf[nq:nq + blk]
    vbuf[0:blk] = vbuf[nq:nq + blk]


def _attn_core(qkv, window, dilation):
    B, d, n, _ = qkv.shape
    blk = ATTN_BLOCK
    assert d == dilation and window // d == blk and n % ATTN_Q_ROWS == 0
    qi = jnp.arange(blk)[:, None]
    kj = jnp.arange(2 * blk)[None, :]
    dist = qi + blk - kj
    band = (dist >= 0) & (dist <= blk)
    neg = jnp.float32(-jnp.inf)
    bias = jnp.stack([jnp.where(band & (kj >= blk), 0.0, neg), jnp.where(band, 0.0, neg)]).astype(F32)

    def spec(width, part):
        return pl.BlockSpec((1, 1, ATTN_Q_ROWS, width), lambda b, r, i: (b, r, i, part))

    lse_width = ATTN_HEADS * LSE_LANES
    return pl.pallas_call(
        _attn_core_kernel,
        grid=(B, d, n // ATTN_Q_ROWS),
        in_specs=[spec(ATTN_WIDTH, 0), spec(ATTN_WIDTH, 1), spec(ATTN_WIDTH, 2), _resident(bias.shape)],
        out_specs=[spec(ATTN_WIDTH, 0), spec(lse_width, 0)],
        out_shape=[jax.ShapeDtypeStruct((B, d, n, ATTN_WIDTH), BF16),
                   jax.ShapeDtypeStruct((B, d, n, lse_width), F32)],
        scratch_shapes=[pltpu.VMEM((ATTN_Q_ROWS + blk, ATTN_WIDTH), BF16),
                        pltpu.VMEM((ATTN_Q_ROWS + blk, ATTN_WIDTH), BF16)],
        compiler_params=_params(3),
        name=f"attn_core_d{d}",
    )(qkv, qkv, qkv, bias)


def _attn_mix(x_ref, o_refs, l_refs, w_ref, ex_ref, obs, lbs):
    tm = x_ref.shape[1]
    outs = [o_refs[0][0, 0].astype(F32)]
    lses = [l_refs[0][0, 0]]
    for o_ref, l_ref, ob, lb in zip(o_refs[1:], l_refs[1:], obs, lbs):
        d = o_ref.shape[1]
        n_lane_tiles = ob.shape[0]
        for r in range(d):
            o_r = o_ref[0, r].astype(F32)
            for c in range(n_lane_tiles):
                ob[c, pl.ds(r, tm // d, stride=d), :] = o_r[:, c * LANES:(c + 1) * LANES]
            lb[pl.ds(r, tm // d, stride=d), :] = l_ref[0, r]
        outs.append(jnp.concatenate([ob[c] for c in range(n_lane_tiles)], axis=1))
        lses.append(lb[...])
    m = jnp.maximum(jnp.maximum(lses[0], lses[1]), lses[2])
    es = [jnp.exp(l - m) for l in lses]
    inv = 1.0 / (es[0] + es[1] + es[2])
    ex = ex_ref[...]
    mix = None
    for e, o in zip(es, outs):
        w = e * inv
        hi = w.astype(BF16)
        lo = (w - hi.astype(F32)).astype(BF16)
        wide = jnp.dot(hi, ex, preferred_element_type=F32) + jnp.dot(lo, ex, preferred_element_type=F32)
        mix = wide * o if mix is None else mix + wide * o
    return x_ref[0] + jnp.dot(mix.astype(BF16), w_ref[...], preferred_element_type=F32)


def _attn_out_ffn_kernel(x_ref, o0, o1, o2, l0, l1, l2, w_ref, ex_ref, g_ref, wup_ref, cw_ref, cb_ref, wdn_ref,
                         out_ref, ob1, ob2, lb1, lb2, *ffn_scratch, nc, ch):
    x = _attn_mix(x_ref, (o0, o1, o2), (l0, l1, l2), w_ref, ex_ref, (ob1, ob2), (lb1, lb2))
    _ffn_body(x, g_ref, wup_ref, cw_ref, cb_ref, wdn_ref, out_ref, *ffn_scratch, nc=nc, ch=ch)


def _attn_out_ffn(x, outs, lses, w_out, ffn_params):
    B, S, D = x.shape
    assert FFN_ROWS == ROW_TILE and S % ROW_TILE == 0
    lse_width = ATTN_HEADS * LSE_LANES
    tile = pl.BlockSpec((1, ROW_TILE, D), lambda b, i: (b, i, 0))

    def part(width, d):
        return pl.BlockSpec((1, d, ROW_TILE // d, width), lambda b, i: (b, 0, i, 0))

    dil = [d for _, d in ATTN_GROUPS]
    assert dil[0] == 1
    lane = jnp.arange(lse_width)[:, None]
    col = jnp.arange(ATTN_WIDTH)[None, :]
    expand = (lane == (col // ATTN_HEAD_DIM) * LSE_LANES).astype(BF16)
    ffn_args, ffn_specs, ffn_scratch, static = _ffn_operands(*ffn_params)
    wide = pltpu.VMEM((ATTN_WIDTH // LANES, ROW_TILE, LANES), F32)
    narrow = pltpu.VMEM((ROW_TILE, lse_width), F32)
    return pl.pallas_call(
        functools.partial(_attn_out_ffn_kernel, **static),
        grid=(B, S // ROW_TILE),
        in_specs=[tile] + [part(ATTN_WIDTH, d) for d in dil] + [part(lse_width, d) for d in dil]
        + [_resident(w_out.shape), _resident(expand.shape)] + ffn_specs,
        out_specs=tile,
        out_shape=jax.ShapeDtypeStruct(x.shape, x.dtype),
        scratch_shapes=[wide, wide, narrow, narrow] + ffn_scratch,
        compiler_params=_params(2),
        name="attn_out_ffn",
    )(x, *outs, *lses, w_out.astype(BF16), expand, *ffn_args)


def _attention_layer(x, gain, w_in, q_gain, k_gain, w_out, ffn_params):
    qkvs = _attn_in(x, gain, w_in, q_gain, k_gain)
    outs, lses = [], []
    for qkv, (window, dilation) in zip(qkvs, ATTN_GROUPS):
        o, lse = _attn_core(qkv, window, dilation)
        outs.append(o)
        lses.append(lse)
    return _attn_out_ffn(x, outs, lses, w_out, ffn_params)


def _conv_kernel(x_ref, g_ref, win_ref, bin_ref, dw_ref, dwb_ref, lng_ref, lnb_ref, wout_ref, bout_ref,
                 o_ref, gbuf, ybuf, *, width):
    halo = CONV_HALO
    tm = x_ref.shape[1]
    C = gbuf.shape[1]

    @pl.when(pl.program_id(1) == 0)
    def _():
        gbuf[0:halo] = jnp.zeros((halo, C), F32)

    x = x_ref[0]
    hb = _rms_norm(x, g_ref[...]).astype(BF16)
    a = jnp.dot(hb, win_ref[:, :C], preferred_element_type=F32) + bin_ref[:, :C]
    gate = jnp.dot(hb, win_ref[:, C:], preferred_element_type=F32) + bin_ref[:, C:]
    gbuf[halo:] = a * _sigmoid(gate)

    first = halo - (width - 1)
    rb, lb = CONV_ROW_BLOCK, CONV_LANE_BLOCK
    nwin = rb + halo

    def row_block(i, carry):
        r0 = pl.multiple_of(i * rb, rb)
        for c0 in range(0, C, lb):
            lanes = slice(c0, c0 + lb)
            win = gbuf[pl.ds(r0, nwin), lanes]
            acc = jnp.zeros((rb, lb), F32) + dwb_ref[:, lanes]
            for phase in range(8):
                taps = [j for j in range(width) if (first + j) % 8 == phase]
                if not taps:
                    continue
                shifted = win if phase == 0 else pltpu.roll(win, nwin - phase, 0)
                for j in taps:
                    a8 = (first + j) // 8 * 8
                    acc = acc + dw_ref[j:j + 1, lanes] * shifted[a8:a8 + rb]
            ybuf[pl.ds(r0, rb), lanes] = acc
        return carry

    lax.fori_loop(0, tm // rb, row_block, 0)
    gbuf[0:halo] = gbuf[tm:tm + halo]

    y = ybuf[...]
    mu = jnp.mean(y, axis=-1, keepdims=True)
    yc = y - mu
    var = jnp.mean(yc * yc, axis=-1, keepdims=True)
    z = yc * lax.rsqrt(var + NORM_EPS) * lng_ref[...] + lnb_ref[...]
    z = (z * _sigmoid(z)).astype(BF16)
    o_ref[0] = x + jnp.dot(z, wout_ref[...], preferred_element_type=F32) + bout_ref[...]


def _conv_mixer(x, gain, w_in, b_in, dw_w, dw_b, ln_g, ln_b, w_out, b_out):
    B, S, D = x.shape
    C = w_out.shape[0]
    width = dw_w.shape[0]
    assert width - 1 <= CONV_HALO
    tile = pl.BlockSpec((1, ROW_TILE, D), lambda b, i: (b, i, 0))
    row = lambda v: v.reshape(1, -1)
    args = (x, row(gain), w_in.astype(BF16), row(b_in), dw_w, row(dw_b), row(ln_g), row(ln_b),
            w_out.astype(BF16), row(b_out))
    return pl.pallas_call(
        functools.partial(_conv_kernel, width=width),
        grid=(B, S // ROW_TILE),
        in_specs=[tile] + [_resident(a.shape) for a in args[1:]],
        out_specs=tile,
        out_shape=jax.ShapeDtypeStruct(x.shape, x.dtype),
        scratch_shapes=[pltpu.VMEM((ROW_TILE + CONV_HALO, C), F32), pltpu.VMEM((ROW_TILE, C), F32)],
        compiler_params=_params(2),
        name="conformer_conv",
    )(*args)


def _hgrn_kernel(x_ref, g_ref, win_ref, lbl_ref, ng_ref, wout_ref, o_ref,
                 qbuf, bbuf, cbuf, vbuf, obuf, state, crow, vrow, *, layer):
    ts = x_ref.shape[1]
    KD = qbuf.shape[1]
    hd = HGRN_HEAD_DIM
    ck = HGRN_CHUNK

    @pl.when(pl.program_id(1) == 0)
    def _():
        state[...] = jnp.zeros_like(state)

    lg = lbl_ref[...]
    e = jnp.exp(lg - jnp.max(lg, axis=0, keepdims=True))
    sm = e / jnp.sum(e, axis=0, keepdims=True)
    lb = jnp.zeros((1, KD), F32)
    for l in range(1, layer + 1):
        lb = lb + sm[l:l + 1]

    x = x_ref[0]
    hb = _rms_norm(x, g_ref[...]).astype(BF16)
    proj = lambda p: jnp.dot(hb, win_ref[:, p * KD:(p + 1) * KD], preferred_element_type=F32)
    q = proj(0)
    qbuf[...] = q * _sigmoid(q)
    fz = proj(1)
    log_sig = jnp.minimum(fz, 0.0) - jnp.log(1.0 + jnp.exp(-jnp.abs(fz)))
    log_lb = jnp.log(lb)
    log_1mlb = jnp.log(1.0 - lb)
    gated = log_1mlb + log_sig
    logf = jnp.maximum(log_lb, gated) + jnp.log(1.0 + jnp.exp(-jnp.abs(log_lb - gated)))
    logk = log_1mlb + (log_sig - fz)
    vbuf[...] = proj(2)
    pos = lax.broadcasted_iota(jnp.int32, (ts, 1), 0) % ck
    sh = 1
    while sh < ck:
        logf = logf + jnp.where(pos >= sh, pltpu.roll(logf, sh, 0), 0.0)
        sh *= 2
    b2 = logf * LOG2E
    bbuf[...] = b2
    cbuf[...] = b2 - logk * LOG2E

    half = ck // 2
    row8 = lax.broadcasted_iota(jnp.int32, (half, 1), 0)
    nt = (((1,), (1,)), ((), ()))
    tn = (((0,), (0,)), ((), ()))

    def chunk(c, carry):
        r0 = pl.multiple_of(c * ck, ck)
        rows = pl.ds(r0, ck)
        for h in range(HGRN_HEADS):
            lanes = slice(h * hd, (h + 1) * hd)
            bc = bbuf[rows, lanes]
            qc = qbuf[rows, lanes]
            cc = cbuf[rows, lanes]
            vc = vbuf[rows, lanes]
            crow[h] = cc
            vrow[h] = vc
            st = state[h]
            o = lax.dot_general((qc * jnp.exp2(bc)).astype(BF16), st.astype(BF16), nt,
                                preferred_element_type=F32)
            halves = []
            for t0 in range(0, ck, half):
                bh, qh, oh = bc[t0:t0 + half], qc[t0:t0 + half], o[t0:t0 + half]
                for s in range(t0 + half):
                    src = pl.ds(s, 1)
                    dec = jnp.exp2(bh - crow[h, src, :])
                    if s > t0:
                        dec = jnp.where(row8 >= s - t0, dec, 0.0)
                    sc = jnp.sum(qh * dec, axis=-1, keepdims=True)
                    oh = oh + sc * vrow[h, src, :]
                halves.append(oh)
            obuf[rows, lanes] = jnp.concatenate(halves, axis=0)
            bl = bc[ck - 1:ck]
            kl = jnp.exp2(bl - cc).astype(BF16)
            state[h] = st * jnp.exp2(bl) + lax.dot_general(vc.astype(BF16), kl, tn,
                                                          preferred_element_type=F32)
        return carry

    lax.fori_loop(0, ts // ck, chunk, 0, unroll=2)

    gz = proj(3)
    gz = gz * _sigmoid(gz)
    for h in range(HGRN_HEADS):
        lanes = slice(h * hd, (h + 1) * hd)
        obuf[:, lanes] = _rms_norm(obuf[:, lanes], ng_ref[:, lanes]) * gz[:, lanes]
    o_ref[0] = x + jnp.dot(obuf[...].astype(BF16), wout_ref[...], preferred_element_type=F32)


def _hgrn_mixer(x, gain, w_in, lb_logits, norm_g, w_out, layer):
    B, S, D = x.shape
    KD = HGRN_HEADS * HGRN_HEAD_DIM
    assert w_in.shape[1] == 4 * KD and w_out.shape[0] == KD and S % HGRN_ROWS == 0
    tile = pl.BlockSpec((1, HGRN_ROWS, D), lambda b, i: (b, i, 0))
    buf = pltpu.VMEM((HGRN_ROWS, KD), F32)
    return pl.pallas_call(
        functools.partial(_hgrn_kernel, layer=layer),
        grid=(B, S // HGRN_ROWS),
        in_specs=[tile, _resident((1, D)), _resident(w_in.shape), _resident(lb_logits.shape),
                  _resident((1, KD)), _resident(w_out.shape)],
        out_specs=tile,
        out_shape=jax.ShapeDtypeStruct(x.shape, x.dtype),
        scratch_shapes=[buf, buf, buf, buf, buf,
                        pltpu.VMEM((HGRN_HEADS, HGRN_HEAD_DIM, HGRN_HEAD_DIM), F32),
                        pltpu.VMEM((HGRN_HEADS, HGRN_CHUNK, HGRN_HEAD_DIM), F32),
                        pltpu.VMEM((HGRN_HEADS, HGRN_CHUNK, HGRN_HEAD_DIM), F32)],
        compiler_params=_params(2),
        name="hgrn2",
    )(x, gain.reshape(1, D), w_in.astype(BF16), lb_logits.astype(F32), norm_g.reshape(1, KD),
      w_out.astype(BF16))


def kernel(x, mixer_norm, ffn_norm, attn_w_in, attn_q_gain, attn_k_gain, attn_w_out, conv_w_in, conv_b_in, conv_dw_w, conv_dw_b, conv_ln_g, conv_ln_b, conv_w_out, conv_b_out, hgrn_w_in, hgrn_lb_logits, hgrn_norm_g, hgrn_w_out, ffn_w_up, ffn_conv_w, ffn_conv_b, ffn_w_down):
    depth = mixer_norm.shape[0]
    for layer in range(depth):
        kind = layer % N_MIXERS
        j = layer // N_MIXERS
        ffn_params = (ffn_norm[layer], ffn_w_up[layer], ffn_conv_w[layer], ffn_conv_b[layer],
                      ffn_w_down[layer])
        if kind == 0:
            x = _attention_layer(x, mixer_norm[layer], attn_w_in[j], attn_q_gain[j], attn_k_gain[j],
                                 attn_w_out[j], ffn_params)
            continue
        if kind == 1:
            x = _conv_mixer(x, mixer_norm[layer], conv_w_in[j], conv_b_in[j], conv_dw_w[j], conv_dw_b[j],
                            conv_ln_g[j], conv_ln_b[j], conv_w_out[j], conv_b_out[j])
        else:
            x = _hgrn_mixer(x, mixer_norm[layer], hgrn_w_in[j], hgrn_lb_logits, hgrn_norm_g[j],
                            hgrn_w_out[j], layer)
        x = _ffn(x, *ffn_params)
    return x
```

```python
import functools

import jax
import jax.numpy as jnp
from jax import lax
from jax.experimental import pallas as pl
from jax.experimental.pallas import tpu as pltpu

F32 = jnp.float32
BF16 = jnp.bfloat16

NORM_EPS = 1e-6
N_MIXERS = 3
LANES = 128
LN2 = 0.6931471805599453
LOG2E = 1.4426950408889634

ATTN_GROUPS = ((128, 1), (512, 4), (2048, 16))
ATTN_HEADS = 8
ATTN_HEAD_DIM = 64
ATTN_BLOCK = 128
ATTN_WIDTH = ATTN_HEADS * ATTN_HEAD_DIM
ATTN_PARTS = 3 * len(ATTN_GROUPS)
ATTN_Q_ROWS = 512
LSE_LANES = 16

HGRN_HEADS = 8
HGRN_HEAD_DIM = 128
HGRN_CHUNK = 16
HGRN_ROWS = 512

CONV_ROW_BLOCK = 64
CONV_LANE_BLOCK = 128
CONV_HALO = 32

FFN_CHUNK = 256
FFN_HALO = 8
FFN_ROWS = 512
ROW_TILE = 512

VMEM_LIMIT = 56 * 1024 * 1024


def _params(n_axes):
    return pltpu.CompilerParams(dimension_semantics=("arbitrary",) * n_axes,
                                vmem_limit_bytes=VMEM_LIMIT)


def _resident(shape):
    zeros = (0,) * len(shape)
    return pl.BlockSpec(shape, lambda *_: zeros, pipeline_mode=pl.Buffered(1))


def _rms_norm(x, gain):
    ms = jnp.mean(x * x, axis=-1, keepdims=True)
    return x * lax.rsqrt(ms + NORM_EPS) * gain


def _sigmoid(x):
    return 1.0 / (1.0 + jnp.exp(-x))


def _shift_rows(u, prev, k):
    r = pltpu.roll(u, k, 0)
    p = pltpu.roll(prev, k, 0)
    row = lax.broadcasted_iota(jnp.int32, prev.shape, 0)
    head = jnp.where(row < k, p, r[:8])
    return jnp.concatenate([head, r[8:]], axis=0)


def _ffn_body(x, g_ref, wup_ref, cw_ref, cb_ref, wdn_ref, o_ref, halo_ref, *, nc, ch):
    @pl.when(pl.program_id(1) == 0)
    def _():
        halo_ref[...] = jnp.zeros_like(halo_ref)

    tm = x.shape[0]
    hb = _rms_norm(x, g_ref[...]).astype(BF16)
    u = jnp.dot(hb, wup_ref[...], preferred_element_type=F32)
    parts = []
    for c in range(nc):
        uc = u[:, c * 2 * ch:(c + 1) * 2 * ch]
        prev = halo_ref[c]
        cw = cw_ref[c]
        y = (uc * cw[2:3] + _shift_rows(uc, prev, 1) * cw[1:2] + _shift_rows(uc, prev, 2) * cw[0:1]
             + cb_ref[c])
        halo_ref[c] = uc[tm - FFN_HALO:]
        gate = y[:, :ch]
        parts.append((gate * _sigmoid(gate) * y[:, ch:]).astype(BF16))
    act = jnp.concatenate(parts, axis=1)
    o_ref[0] = x + jnp.dot(act, wdn_ref[...], preferred_element_type=F32)


def _ffn_kernel(x_ref, *refs, nc, ch):
    _ffn_body(x_ref[0], *refs, nc=nc, ch=ch)


def _ffn_operands(gain, w_up, conv_w, conv_b, w_down):
    D = w_up.shape[0]
    F = w_down.shape[0]
    ch = FFN_CHUNK
    nc = F // ch
    assert nc * ch == F and nc >= 2
    wup = w_up.astype(BF16).reshape(D, 2, nc, ch).transpose(0, 2, 1, 3).reshape(D, 2 * F)
    cw = conv_w.reshape(3, 2, nc, ch).transpose(2, 0, 1, 3).reshape(nc, 3, 2 * ch)
    cb = conv_b.reshape(2, nc, ch).transpose(1, 0, 2).reshape(nc, 1, 2 * ch)
    args = (gain.reshape(1, D), wup, cw, cb, w_down.astype(BF16))
    scratch = [pltpu.VMEM((nc, FFN_HALO, 2 * ch), F32)]
    return args, [_resident(a.shape) for a in args], scratch, dict(nc=nc, ch=ch)


def _ffn(x, gain, w_up, conv_w, conv_b, w_down):
    B, S, D = x.shape
    assert S % FFN_ROWS == 0
    args, specs, scratch, static = _ffn_operands(gain, w_up, conv_w, conv_b, w_down)
    tile = pl.BlockSpec((1, FFN_ROWS, D), lambda b, i: (b, i, 0))
    return pl.pallas_call(
        functools.partial(_ffn_kernel, **static),
        grid=(B, S // FFN_ROWS),
        in_specs=[tile] + specs,
        out_specs=tile,
        out_shape=jax.ShapeDtypeStruct(x.shape, x.dtype),
        scratch_shapes=scratch,
        compiler_params=_params(2),
        name="ffn",
    )(x, *args)


def _attn_in_kernel(x_ref, g_ref, w_ref, qk_ref, bd_ref, o0_ref, o1_ref, o2_ref, ybuf):
    outs = (o0_ref, o1_ref, o2_ref)
    tm = x_ref.shape[1]
    hb = _rms_norm(x_ref[0], g_ref[...]).astype(BF16)
    bd = bd_ref[...]
    half = bd.shape[0]
    n_lane_tiles = ATTN_WIDTH // LANES
    for j in range(ATTN_PARTS):
        g, part = divmod(j, 3)
        d = ATTN_GROUPS[g][1]
        acc = jnp.dot(hb, w_ref[:, j * ATTN_WIDTH:(j + 1) * ATTN_WIDTH], preferred_element_type=F32)
        if part != 2:
            sq = (acc * acc).astype(BF16)
            ssq = jnp.concatenate(
                [jnp.dot(sq[:, :half], bd, preferred_element_type=F32),
                 jnp.dot(sq[:, half:], bd, preferred_element_type=F32)], axis=1)
            acc = acc * lax.rsqrt(ssq * (1.0 / ATTN_HEAD_DIM) + NORM_EPS) * qk_ref[j:j + 1, :]
        cols = slice(part * ATTN_WIDTH, (part + 1) * ATTN_WIDTH)
        if d == 1:
            outs[g][0, 0, :, cols] = acc.astype(BF16)
        else:
            for c in range(n_lane_tiles):
                ybuf[c] = acc[:, c * LANES:(c + 1) * LANES]
            for r in range(d):
                piece = [ybuf[c, pl.ds(r, tm // d, stride=d), :] for c in range(n_lane_tiles)]
                outs[g][0, r, :, cols] = jnp.concatenate(piece, axis=1).astype(BF16)


def _attn_in(x, gain, w_in, q_gain, k_gain):
    B, S, D = x.shape
    W = w_in.shape[1]
    scale = ATTN_HEAD_DIM ** -0.5 * LOG2E
    rows = []
    for g in range(len(ATTN_GROUPS)):
        rows += [jnp.tile(q_gain[g], ATTN_HEADS) * scale, jnp.tile(k_gain[g], ATTN_HEADS),
                 jnp.ones((ATTN_WIDTH,), F32)]
    qk = jnp.stack(rows).astype(F32)
    half = ATTN_WIDTH // 2
    head = jnp.arange(half) // ATTN_HEAD_DIM
    bd = (head[:, None] == head[None, :]).astype(BF16)
    out_specs, out_shape = [], []
    for _, d in ATTN_GROUPS:
        assert ROW_TILE % (16 * d) == 0
        out_specs.append(pl.BlockSpec((1, d, ROW_TILE // d, 3 * ATTN_WIDTH), lambda b, i: (b, 0, i, 0)))
        out_shape.append(jax.ShapeDtypeStruct((B, d, S // d, 3 * ATTN_WIDTH), BF16))
    return pl.pallas_call(
        _attn_in_kernel,
        grid=(B, S // ROW_TILE),
        in_specs=[pl.BlockSpec((1, ROW_TILE, D), lambda b, i: (b, i, 0)), _resident((1, D)),
                  _resident((D, W)), _resident(qk.shape), _resident(bd.shape)],
        out_specs=out_specs,
        out_shape=out_shape,
        scratch_shapes=[pltpu.VMEM((ATTN_WIDTH // LANES, ROW_TILE, LANES), F32)],
        compiler_params=_params(2),
        name="attn_in",
    )(x, gain.reshape(1, D), w_in.astype(BF16), qk, bd)


def _attn_core_kernel(q_ref, k_ref, v_ref, bias_ref, o_ref, lse_ref, kbuf, vbuf):
    blk = ATTN_BLOCK
    nq = q_ref.shape[2]
    first = pl.program_id(2) == 0

    @pl.when(first)
    def _():
        kbuf[0:blk] = jnp.zeros((blk, ATTN_WIDTH), BF16)
        vbuf[0:blk] = jnp.zeros((blk, ATTN_WIDTH), BF16)

    kbuf[blk:] = k_ref[0, 0]
    vbuf[blk:] = v_ref[0, 0]
    lo = lax.broadcasted_iota(jnp.int32, (1, 2 * ATTN_HEAD_DIM), 1) < ATTN_HEAD_DIM
    lse_head = lax.broadcasted_iota(jnp.int32, (1, ATTN_HEADS * LSE_LANES), 1) // LSE_LANES
    zero = jnp.zeros((), BF16)
    nt = (((1,), (1,)), ((), ()))
    for jq in range(nq // blk):
        bias = bias_ref[1]
        if jq == 0:
            bias = jnp.where(first, bias_ref[0], bias)
        rows = slice(jq * blk, (jq + 1) * blk)
        lse = jnp.zeros((blk, ATTN_HEADS * LSE_LANES), F32)
        for hp in range(ATTN_HEADS // 2):
            lanes = slice(hp * 2 * ATTN_HEAD_DIM, (hp + 1) * 2 * ATTN_HEAD_DIM)
            q2 = q_ref[0, 0, rows, lanes]
            kk = kbuf[jq * blk:(jq + 2) * blk, lanes]
            vv = vbuf[jq * blk:(jq + 2) * blk, lanes]
            qq = jnp.concatenate([jnp.where(lo, q2, zero), jnp.where(lo, zero, q2)], axis=0)
            s2 = lax.dot_general(qq, kk, nt, preferred_element_type=F32)
            ps, invs = [], []
            for e in range(2):
                s = s2[e * blk:(e + 1) * blk] + bias
                m = jnp.max(s, axis=-1, keepdims=True)
                p = jnp.exp2(s - m)
                l = jnp.sum(p, axis=-1, keepdims=True)
                ps.append(p.astype(BF16))
                invs.append(1.0 / l)
                lse = jnp.where(lse_head == 2 * hp + e, m * LN2 + jnp.log(l), lse)
            pcat = jnp.concatenate(ps, axis=1)
            vcat = jnp.concatenate([jnp.where(lo, vv, zero), jnp.where(lo, zero, vv)], axis=0)
            o2 = jnp.dot(pcat, vcat, preferred_element_type=F32)
            o_ref[0, 0, rows, lanes] = (o2 * jnp.where(lo, invs[0], invs[1])).astype(o_ref.dtype)
        lse_ref[0, 0, rows, :] = lse
    kbuf[0:blk] = kbuf[nq:nq + blk]
    vbuf[0:blk] = vbuf[nq:nq + blk]


def _attn_core(qkv, window, dilation):
    B, d, n, _ = qkv.shape
    blk = ATTN_BLOCK
    assert d == dilation and window // d == blk and n % ATTN_Q_ROWS == 0
    qi = jnp.arange(blk)[:, None]
    kj = jnp.arange(2 * blk)[None, :]
    dist = qi + blk - kj
    band = (dist >= 0) & (dist <= blk)
    neg = jnp.float32(-jnp.inf)
    bias = jnp.stack([jnp.where(band & (kj >= blk), 0.0, neg), jnp.where(band, 0.0, neg)]).astype(F32)

    def spec(width, part):
        return pl.BlockSpec((1, 1, ATTN_Q_ROWS, width), lambda b, r, i: (b, r, i, part))

    lse_width = ATTN_HEADS * LSE_LANES
    return pl.pallas_call(
        _attn_core_kernel,
        grid=(B, d, n // ATTN_Q_ROWS),
        in_specs=[spec(ATTN_WIDTH, 0), spec(ATTN_WIDTH, 1), spec(ATTN_WIDTH, 2), _resident(bias.shape)],
        out_specs=[spec(ATTN_WIDTH, 0), spec(lse_width, 0)],
        out_shape=[jax.ShapeDtypeStruct((B, d, n, ATTN_WIDTH), BF16),
                   jax.ShapeDtypeStruct((B, d, n, lse_width), F32)],
        scratch_shapes=[pltpu.VMEM((ATTN_Q_ROWS + blk, ATTN_WIDTH), BF16),
                        pltpu.VMEM((ATTN_Q_ROWS + blk, ATTN_WIDTH), BF16)],
        compiler_params=_params(3),
        name=f"attn_core_d{d}",
    )(qkv, qkv, qkv, bias)


def _attn_mix(x_ref, o_refs, l_refs, w_ref, ex_ref, obs, lbs):
    tm = x_ref.shape[1]
    outs = [o_refs[0][0, 0].astype(F32)]
    lses = [l_refs[0][0, 0]]
    for o_ref, l_ref, ob, lb in zip(o_refs[1:], l_refs[1:], obs, lbs):
        d = o_ref.shape[1]
        n_lane_tiles = ob.shape[0]
        for r in range(d):
            o_r = o_ref[0, r].astype(F32)
            for c in range(n_lane_tiles):
                ob[c, pl.ds(r, tm // d, stride=d), :] = o_r[:, c * LANES:(c + 1) * LANES]
            lb[pl.ds(r, tm // d, stride=d), :] = l_ref[0, r]
        outs.append(jnp.concatenate([ob[c] for c in range(n_lane_tiles)], axis=1))
        lses.append(lb[...])
    m = jnp.maximum(jnp.maximum(lses[0], lses[1]), lses[2])
    es = [jnp.exp(l - m) for l in lses]
    inv = 1.0 / (es[0] + es[1] + es[2])
    ex = ex_ref[...]
    mix = None
    for e, o in zip(es, outs):
        w = e * inv
        hi = w.astype(BF16)
        lo = (w - hi.astype(F32)).astype(BF16)
        wide = jnp.dot(hi, ex, preferred_element_type=F32) + jnp.dot(lo, ex, preferred_element_type=F32)
        mix = wide * o if mix is None else mix + wide * o
    return x_ref[0] + jnp.dot(mix.astype(BF16), w_ref[...], preferred_element_type=F32)


def _attn_out_ffn_kernel(x_ref, o0, o1, o2, l0, l1, l2, w_ref, ex_ref, g_ref, wup_ref, cw_ref, cb_ref, wdn_ref,
                         out_ref, ob1, ob2, lb1, lb2, *ffn_scratch, nc, ch):
    x = _attn_mix(x_ref, (o0, o1, o2), (l0, l1, l2), w_ref, ex_ref, (ob1, ob2), (lb1, lb2))
    _ffn_body(x, g_ref, wup_ref, cw_ref, cb_ref, wdn_ref, out_ref, *ffn_scratch, nc=nc, ch=ch)


def _attn_out_ffn(x, outs, lses, w_out, ffn_params):
    B, S, D = x.shape
    assert FFN_ROWS == ROW_TILE and S % ROW_TILE == 0
    lse_width = ATTN_HEADS * LSE_LANES
    tile = pl.BlockSpec((1, ROW_TILE, D), lambda b, i: (b, i, 0))

    def part(width, d):
        return pl.BlockSpec((1, d, ROW_TILE // d, width), lambda b, i: (b, 0, i, 0))

    dil = [d for _, d in ATTN_GROUPS]
    assert dil[0] == 1
    lane = jnp.arange(lse_width)[:, None]
    col = jnp.arange(ATTN_WIDTH)[None, :]
    expand = (lane == (col // ATTN_HEAD_DIM) * LSE_LANES).astype(BF16)
    ffn_args, ffn_specs, ffn_scratch, static = _ffn_operands(*ffn_params)
    wide = pltpu.VMEM((ATTN_WIDTH // LANES, ROW_TILE, LANES), F32)
    narrow = pltpu.VMEM((ROW_TILE, lse_width), F32)
    return pl.pallas_call(
        functools.partial(_attn_out_ffn_kernel, **static),
        grid=(B, S // ROW_TILE),
        in_specs=[tile] + [part(ATTN_WIDTH, d) for d in dil] + [part(lse_width, d) for d in dil]
        + [_resident(w_out.shape), _resident(expand.shape)] + ffn_specs,
        out_specs=tile,
        out_shape=jax.ShapeDtypeStruct(x.shape, x.dtype),
        scratch_shapes=[wide, wide, narrow, narrow] + ffn_scratch,
        compiler_params=_params(2),
        name="attn_out_ffn",
    )(x, *outs, *lses, w_out.astype(BF16), expand, *ffn_args)


def _attention_layer(x, gain, w_in, q_gain, k_gain, w_out, ffn_params):
    qkvs = _attn_in(x, gain, w_in, q_gain, k_gain)
    outs, lses = [], []
    for qkv, (window, dilation) in zip(qkvs, ATTN_GROUPS):
        o, lse = _attn_core(qkv, window, dilation)
        outs.append(o)
        lses.append(lse)
    return _attn_out_ffn(x, outs, lses, w_out, ffn_params)


def _conv_kernel(x_ref, g_ref, win_ref, bin_ref, dw_ref, dwb_ref, lng_ref, lnb_ref, wout_ref, bout_ref,
                 o_ref, gbuf, ybuf, *, width):
    halo = CONV_HALO
    tm = x_ref.shape[1]
    C = gbuf.shape[1]

    @pl.when(pl.program_id(1) == 0)
    def _():
        gbuf[0:halo] = jnp.zeros((halo, C), F32)

    x = x_ref[0]
    hb = _rms_norm(x, g_ref[...]).astype(BF16)
    u = jnp.dot(hb, win_ref[...], preferred_element_type=F32) + bin_ref[...]
    gbuf[halo:] = u[:, :C] * _sigmoid(u[:, C:])

    first = halo - (width - 1)
    rb, lb = CONV_ROW_BLOCK, CONV_LANE_BLOCK
    nwin = rb + halo

    def row_block(i, carry):
        r0 = pl.multiple_of(i * rb, rb)
        for c0 in range(0, C, lb):
            lanes = slice(c0, c0 + lb)
            win = gbuf[pl.ds(r0, nwin), lanes]
            acc = jnp.zeros((rb, lb), F32) + dwb_ref[:, lanes]
            for phase in range(8):
                taps = [j for j in range(width) if (first + j) % 8 == phase]
                if not taps:
                    continue
                shifted = win if phase == 0 else pltpu.roll(win, nwin - phase, 0)
                for j in taps:
                    a8 = (first + j) // 8 * 8
                    acc = acc + dw_ref[j:j + 1, lanes] * shifted[a8:a8 + rb]
            ybuf[pl.ds(r0, rb), lanes] = acc
        return carry

    lax.fori_loop(0, tm // rb, row_block, 0)
    gbuf[0:halo] = gbuf[tm:tm + halo]

    y = ybuf[...]
    mu = jnp.mean(y, axis=-1, keepdims=True)
    yc = y - mu
    var = jnp.mean(yc * yc, axis=-1, keepdims=True)
    z = yc * lax.rsqrt(var + NORM_EPS) * lng_ref[...] + lnb_ref[...]
    z = (z * _sigmoid(z)).astype(BF16)
    o_ref[0] = x + jnp.dot(z, wout_ref[...], preferred_element_type=F32) + bout_ref[...]


def _conv_mixer(x, gain, w_in, b_in, dw_w, dw_b, ln_g, ln_b, w_out, b_out):
    B, S, D = x.shape
    C = w_out.shape[0]
    width = dw_w.shape[0]
    assert width - 1 <= CONV_HALO
    tile = pl.BlockSpec((1, ROW_TILE, D), lambda b, i: (b, i, 0))
    row = lambda v: v.reshape(1, -1)
    args = (x, row(gain), w_in.astype(BF16), row(b_in), dw_w, row(dw_b), row(ln_g), row(ln_b),
            w_out.astype(BF16), row(b_out))
    return pl.pallas_call(
        functools.partial(_conv_kernel, width=width),
        grid=(B, S // ROW_TILE),
        in_specs=[tile] + [_resident(a.shape) for a in args[1:]],
        out_specs=tile,
        out_shape=jax.ShapeDtypeStruct(x.shape, x.dtype),
        scratch_shapes=[pltpu.VMEM((ROW_TILE + CONV_HALO, C), F32), pltpu.VMEM((ROW_TILE, C), F32)],
        compiler_params=_params(2),
        name="conformer_conv",
    )(*args)


def _hgrn_kernel(x_ref, g_ref, win_ref, lbl_ref, ng_ref, wout_ref, o_ref,
                 qbuf, bbuf, cbuf, vbuf, obuf, state, crow, vrow, *, layer):
    ts = x_ref.shape[1]
    KD = qbuf.shape[1]
    hd = HGRN_HEAD_DIM
    ck = HGRN_CHUNK

    @pl.when(pl.program_id(1) == 0)
    def _():
        state[...] = jnp.zeros_like(state)

    lg = lbl_ref[...]
    e = jnp.exp(lg - jnp.max(lg, axis=0, keepdims=True))
    sm = e / jnp.sum(e, axis=0, keepdims=True)
    lb = jnp.zeros((1, KD), F32)
    for l in range(1, layer + 1):
        lb = lb + sm[l:l + 1]

    x = x_ref[0]
    hb = _rms_norm(x, g_ref[...]).astype(BF16)
    proj = lambda p: jnp.dot(hb, win_ref[:, p * KD:(p + 1) * KD], preferred_element_type=F32)
    q = proj(0)
    qbuf[...] = q * _sigmoid(q)
    fz = proj(1)
    log_sig = jnp.minimum(fz, 0.0) - jnp.log(1.0 + jnp.exp(-jnp.abs(fz)))
    log_lb = jnp.log(lb)
    log_1mlb = jnp.log(1.0 - lb)
    gated = log_1mlb + log_sig
    logf = jnp.maximum(log_lb, gated) + jnp.log(1.0 + jnp.exp(-jnp.abs(log_lb - gated)))
    logk = log_1mlb + (log_sig - fz)
    vbuf[...] = proj(2)
    pos = lax.broadcasted_iota(jnp.int32, (ts, 1), 0) % ck
    sh = 1
    while sh < ck:
        logf = logf + jnp.where(pos >= sh, pltpu.roll(logf, sh, 0), 0.0)
        sh *= 2
    b2 = logf * LOG2E
    bbuf[...] = b2
    cbuf[...] = b2 - logk * LOG2E

    half = ck // 2
    row8 = lax.broadcasted_iota(jnp.int32, (half, 1), 0)
    nt = (((1,), (1,)), ((), ()))
    tn = (((0,), (0,)), ((), ()))

    def chunk(c, carry):
        r0 = pl.multiple_of(c * ck, ck)
        rows = pl.ds(r0, ck)
        for h in range(HGRN_HEADS):
            lanes = slice(h * hd, (h + 1) * hd)
            bc = bbuf[rows, lanes]
            qc = qbuf[rows, lanes]
            cc = cbuf[rows, lanes]
            vc = vbuf[rows, lanes]
            crow[h] = cc
            vrow[h] = vc
            st = state[h]
            o = lax.dot_general((qc * jnp.exp2(bc)).astype(BF16), st.astype(BF16), nt,
                                preferred_element_type=F32)
            halves = []
            for t0 in range(0, ck, half):
                bh, qh, oh = bc[t0:t0 + half], qc[t0:t0 + half], o[t0:t0 + half]
                for s in range(t0 + half):
                    src = pl.ds(s, 1)
                    dec = jnp.exp2(bh - crow[h, src, :])
                    if s > t0:
                        dec = jnp.where(row8 >= s - t0, dec, 0.0)
                    sc = jnp.sum(qh * dec, axis=-1, keepdims=True)
                    oh = oh + sc * vrow[h, src, :]
                halves.append(oh)
            obuf[rows, lanes] = jnp.concatenate(halves, axis=0)
            bl = bc[ck - 1:ck]
            kl = jnp.exp2(bl - cc).astype(BF16)
            state[h] = st * jnp.exp2(bl) + lax.dot_general(vc.astype(BF16), kl, tn,
                                                          preferred_element_type=F32)
        return carry

    lax.fori_loop(0, ts // ck, chunk, 0, unroll=2)

    gz = proj(3)
    gz = gz * _sigmoid(gz)
    for h in range(HGRN_HEADS):
        lanes = slice(h * hd, (h + 1) * hd)
        obuf[:, lanes] = _rms_norm(obuf[:, lanes], ng_ref[:, lanes]) * gz[:, lanes]
    o_ref[0] = x + jnp.dot(obuf[...].astype(BF16), wout_ref[...], preferred_element_type=F32)


def _hgrn_mixer(x, gain, w_in, lb_logits, norm_g, w_out, layer):
    B, S, D = x.shape
    KD = HGRN_HEADS * HGRN_HEAD_DIM
    assert w_in.shape[1] == 4 * KD and w_out.shape[0] == KD and S % HGRN_ROWS == 0
    tile = pl.BlockSpec((1, HGRN_ROWS, D), lambda b, i: (b, i, 0))
    buf = pltpu.VMEM((HGRN_ROWS, KD), F32)
    return pl.pallas_call(
        functools.partial(_hgrn_kernel, layer=layer),
        grid=(B, S // HGRN_ROWS),
        in_specs=[tile, _resident((1, D)), _resident(w_in.shape), _resident(lb_logits.shape),
                  _resident((1, KD)), _resident(w_out.shape)],
        out_specs=tile,
        out_shape=jax.ShapeDtypeStruct(x.shape, x.dtype),
        scratch_shapes=[buf, buf, buf, buf, buf,
                        pltpu.VMEM((HGRN_HEADS, HGRN_HEAD_DIM, HGRN_HEAD_DIM), F32),
                        pltpu.VMEM((HGRN_HEADS, HGRN_CHUNK, HGRN_HEAD_DIM), F32),
                        pltpu.VMEM((HGRN_HEADS, HGRN_CHUNK, HGRN_HEAD_DIM), F32)],
        compiler_params=_params(2),
        name="hgrn2",
    )(x, gain.reshape(1, D), w_in.astype(BF16), lb_logits.astype(F32), norm_g.reshape(1, KD),
      w_out.astype(BF16))


def kernel(x, mixer_norm, ffn_norm, attn_w_in, attn_q_gain, attn_k_gain, attn_w_out, conv_w_in, conv_b_in, conv_dw_w, conv_dw_b, conv_ln_g, conv_ln_b, conv_w_out, conv_b_out, hgrn_w_in, hgrn_lb_logits, hgrn_norm_g, hgrn_w_out, ffn_w_up, ffn_conv_w, ffn_conv_b, ffn_w_down):
    depth = mixer_norm.shape[0]
    for layer in range(depth):
        kind = layer % N_MIXERS
        j = layer // N_MIXERS
        ffn_params = (ffn_norm[layer], ffn_w_up[layer], ffn_conv_w[layer], ffn_conv_b[layer],
                      ffn_w_down[layer])
        if kind == 0:
            x = _attention_layer(x, mixer_norm[layer], attn_w_in[j], attn_q_gain[j], attn_k_gain[j],
                                 attn_w_out[j], ffn_params)
            continue
        if kind == 1:
            x = _conv_mixer(x, mixer_norm[layer], conv_w_in[j], conv_b_in[j], conv_dw_w[j], conv_dw_b[j],
                            conv_ln_g[j], conv_ln_b[j], conv_w_out[j], conv_b_out[j])
        else:
            x = _hgrn_mixer(x, mixer_norm[layer], hgrn_w_in[j], hgrn_lb_logits, hgrn_norm_g[j],
                            hgrn_w_out[j], layer)
        x = _ffn(x, *ffn_params)
    return x
```

```python
import functools

import jax
import jax.numpy as jnp
from jax import lax
from jax.experimental import pallas as pl
from jax.experimental.pallas import tpu as pltpu

F32 = jnp.float32
BF16 = jnp.bfloat16

NORM_EPS = 1e-6
N_MIXERS = 3
LANES = 128
LN2 = 0.6931471805599453
LOG2E = 1.4426950408889634

ATTN_GROUPS = ((128, 1), (512, 4), (2048, 16))
ATTN_HEADS = 8
ATTN_HEAD_DIM = 64
ATTN_BLOCK = 128
ATTN_WIDTH = ATTN_HEADS * ATTN_HEAD_DIM
ATTN_PARTS = 3 * len(ATTN_GROUPS)
ATTN_Q_ROWS = 1024
LSE_LANES = 16

HGRN_HEADS = 8
HGRN_HEAD_DIM = 128
HGRN_CHUNK = 16
HGRN_ROWS = 512

CONV_ROW_BLOCK = 64
CONV_LANE_BLOCK = 128
CONV_HALO = 32

FFN_CHUNK = 256
FFN_HALO = 8
FFN_ROWS = 512
ROW_TILE = 512

VMEM_LIMIT = 56 * 1024 * 1024


def _params(n_axes):
    return pltpu.CompilerParams(dimension_semantics=("arbitrary",) * n_axes,
                                vmem_limit_bytes=VMEM_LIMIT)


def _resident(shape):
    zeros = (0,) * len(shape)
    return pl.BlockSpec(shape, lambda *_: zeros, pipeline_mode=pl.Buffered(1))


def _rms_norm(x, gain):
    ms = jnp.mean(x * x, axis=-1, keepdims=True)
    return x * lax.rsqrt(ms + NORM_EPS) * gain


def _sigmoid(x):
    return 1.0 / (1.0 + jnp.exp2(x * -LOG2E))


def _shift_rows(u, prev, k):
    r = pltpu.roll(u, k, 0)
    p = pltpu.roll(prev, k, 0)
    row = lax.broadcasted_iota(jnp.int32, prev.shape, 0)
    head = jnp.where(row < k, p, r[:8])
    return jnp.concatenate([head, r[8:]], axis=0)


def _ffn_body(x, g_ref, wup_ref, cw_ref, cb_ref, wdn_ref, o_ref, halo_ref, *, nc, ch):
    @pl.when(pl.program_id(1) == 0)
    def _():
        halo_ref[...] = jnp.zeros_like(halo_ref)

    tm = x.shape[0]
    hb = _rms_norm(x, g_ref[...]).astype(BF16)
    u = jnp.dot(hb, wup_ref[...], preferred_element_type=F32)
    parts = []
    for c in range(nc):
        uc = u[:, c * 2 * ch:(c + 1) * 2 * ch]
        prev = halo_ref[c]
        cw = cw_ref[c]
        y = (uc * cw[2:3] + _shift_rows(uc, prev, 1) * cw[1:2] + _shift_rows(uc, prev, 2) * cw[0:1]
             + cb_ref[c])
        halo_ref[c] = uc[tm - FFN_HALO:]
        gate = y[:, :ch]
        parts.append((gate * _sigmoid(gate) * y[:, ch:]).astype(BF16))
    act = jnp.concatenate(parts, axis=1)
    o_ref[0] = x + jnp.dot(act, wdn_ref[...], preferred_element_type=F32)


def _ffn_kernel(x_ref, *refs, nc, ch):
    _ffn_body(x_ref[0], *refs, nc=nc, ch=ch)


def _pack_w_up_kernel(gate_ref, up_ref, o_ref):
    ch = gate_ref.shape[1]
    o_ref[:, :ch] = gate_ref[...].astype(BF16)
    o_ref[:, ch:] = up_ref[...].astype(BF16)


def _pack_w_up(w_up, nc, ch):
    D = w_up.shape[0]
    return pl.pallas_call(
        _pack_w_up_kernel,
        grid=(nc,),
        in_specs=[pl.BlockSpec((D, ch), lambda c: (0, c)), pl.BlockSpec((D, ch), lambda c: (0, nc + c))],
        out_specs=pl.BlockSpec((D, 2 * ch), lambda c: (0, c)),
        out_shape=jax.ShapeDtypeStruct(w_up.shape, BF16),
        compiler_params=_params(1),
        name="pack_w_up",
    )(w_up, w_up)


def _ffn_operands(gain, w_up, conv_w, conv_b, w_down):
    D = w_up.shape[0]
    F = w_down.shape[0]
    ch = FFN_CHUNK
    nc = F // ch
    assert nc * ch == F and nc >= 2
    wup = _pack_w_up(w_up, nc, ch)
    cw = conv_w.reshape(3, 2, nc, ch).transpose(2, 0, 1, 3).reshape(nc, 3, 2 * ch)
    cb = conv_b.reshape(2, nc, ch).transpose(1, 0, 2).reshape(nc, 1, 2 * ch)
    args = (gain.reshape(1, D), wup, cw, cb, w_down.astype(BF16))
    scratch = [pltpu.VMEM((nc, FFN_HALO, 2 * ch), F32)]
    return args, [_resident(a.shape) for a in args], scratch, dict(nc=nc, ch=ch)


def _ffn(x, gain, w_up, conv_w, conv_b, w_down):
    B, S, D = x.shape
    assert S % FFN_ROWS == 0
    args, specs, scratch, static = _ffn_operands(gain, w_up, conv_w, conv_b, w_down)
    tile = pl.BlockSpec((1, FFN_ROWS, D), lambda b, i: (b, i, 0))
    return pl.pallas_call(
        functools.partial(_ffn_kernel, **static),
        grid=(B, S // FFN_ROWS),
        in_specs=[tile] + specs,
        out_specs=tile,
        out_shape=jax.ShapeDtypeStruct(x.shape, x.dtype),
        scratch_shapes=scratch,
        compiler_params=_params(2),
        name="ffn",
    )(x, *args)


def _attn_in_kernel(x_ref, g_ref, w_ref, qk_ref, bd_ref, o0_ref, o1_ref, o2_ref, ybuf):
    outs = (o0_ref, o1_ref, o2_ref)
    tm = x_ref.shape[1]
    hb = _rms_norm(x_ref[0], g_ref[...]).astype(BF16)
    bd = bd_ref[...]
    half = bd.shape[0]
    n_lane_tiles = ATTN_WIDTH // LANES
    for j in range(ATTN_PARTS):
        g, part = divmod(j, 3)
        d = ATTN_GROUPS[g][1]
        acc = jnp.dot(hb, w_ref[:, j * ATTN_WIDTH:(j + 1) * ATTN_WIDTH], preferred_element_type=F32)
        if part != 2:
            sq = (acc * acc).astype(BF16)
            ssq = jnp.concatenate(
                [jnp.dot(sq[:, :half], bd, preferred_element_type=F32),
                 jnp.dot(sq[:, half:], bd, preferred_element_type=F32)], axis=1)
            acc = acc * lax.rsqrt(ssq * (1.0 / ATTN_HEAD_DIM) + NORM_EPS) * qk_ref[j:j + 1, :]
        cols = slice(part * ATTN_WIDTH, (part + 1) * ATTN_WIDTH)
        if d == 1:
            outs[g][0, 0, :, cols] = acc.astype(BF16)
        else:
            for c in range(n_lane_tiles):
                ybuf[c] = acc[:, c * LANES:(c + 1) * LANES]
            for r in range(d):
                piece = [ybuf[c, pl.ds(r, tm // d, stride=d), :] for c in range(n_lane_tiles)]
                outs[g][0, r, :, cols] = jnp.concatenate(piece, axis=1).astype(BF16)


def _attn_in(x, gain, w_in, q_gain, k_gain):
    B, S, D = x.shape
    W = w_in.shape[1]
    scale = ATTN_HEAD_DIM ** -0.5 * LOG2E
    rows = []
    for g in range(len(ATTN_GROUPS)):
        rows += [jnp.tile(q_gain[g], ATTN_HEADS) * scale, jnp.tile(k_gain[g], ATTN_HEADS),
                 jnp.ones((ATTN_WIDTH,), F32)]
    qk = jnp.stack(rows).astype(F32)
    half = ATTN_WIDTH // 2
    head = jnp.arange(half) // ATTN_HEAD_DIM
    bd = (head[:, None] == head[None, :]).astype(BF16)
    out_specs, out_shape = [], []
    for _, d in ATTN_GROUPS:
        assert ROW_TILE % (16 * d) == 0
        out_specs.append(pl.BlockSpec((1, d, ROW_TILE // d, 3 * ATTN_WIDTH), lambda b, i: (b, 0, i, 0)))
        out_shape.append(jax.ShapeDtypeStruct((B, d, S // d, 3 * ATTN_WIDTH), BF16))
    return pl.pallas_call(
        _attn_in_kernel,
        grid=(B, S // ROW_TILE),
        in_specs=[pl.BlockSpec((1, ROW_TILE, D), lambda b, i: (b, i, 0)), _resident((1, D)),
                  _resident((D, W)), _resident(qk.shape), _resident(bd.shape)],
        out_specs=out_specs,
        out_shape=out_shape,
        scratch_shapes=[pltpu.VMEM((ATTN_WIDTH // LANES, ROW_TILE, LANES), F32)],
        compiler_params=_params(2),
        name="attn_in",
    )(x, gain.reshape(1, D), w_in.astype(BF16), qk, bd)


def _attn_core_kernel(q_ref, k_ref, v_ref, bias_ref, o_ref, lse_ref, kbuf, vbuf):
    blk = ATTN_BLOCK
    nq = q_ref.shape[2]
    first = pl.program_id(2) == 0

    @pl.when(first)
    def _():
        kbuf[0:blk] = jnp.zeros((blk, ATTN_WIDTH), BF16)
        vbuf[0:blk] = jnp.zeros((blk, ATTN_WIDTH), BF16)

    kbuf[blk:] = k_ref[0, 0]
    vbuf[blk:] = v_ref[0, 0]
    lo = lax.broadcasted_iota(jnp.int32, (1, 2 * ATTN_HEAD_DIM), 1) < ATTN_HEAD_DIM
    lse_head = lax.broadcasted_iota(jnp.int32, (1, ATTN_HEADS * LSE_LANES), 1) // LSE_LANES
    zero = jnp.zeros((), BF16)
    nt = (((1,), (1,)), ((), ()))
    for jq in range(nq // blk):
        bias = bias_ref[1]
        if jq == 0:
            bias = jnp.where(first, bias_ref[0], bias)
        rows = slice(jq * blk, (jq + 1) * blk)
        lse = jnp.zeros((blk, ATTN_HEADS * LSE_LANES), F32)
        for hp in range(ATTN_HEADS // 2):
            lanes = slice(hp * 2 * ATTN_HEAD_DIM, (hp + 1) * 2 * ATTN_HEAD_DIM)
            q2 = q_ref[0, 0, rows, lanes]
            kk = kbuf[jq * blk:(jq + 2) * blk, lanes]
            vv = vbuf[jq * blk:(jq + 2) * blk, lanes]
            qq = jnp.concatenate([jnp.where(lo, q2, zero), jnp.where(lo, zero, q2)], axis=0)
            s2 = lax.dot_general(qq, kk, nt, preferred_element_type=F32)
            ps, invs = [], []
            for e in range(2):
                s = s2[e * blk:(e + 1) * blk] + bias
                m = jnp.max(s, axis=-1, keepdims=True)
                p = jnp.exp2(s - m)
                l = jnp.sum(p, axis=-1, keepdims=True)
                ps.append(p.astype(BF16))
                invs.append(1.0 / l)
                lse = jnp.where(lse_head == 2 * hp + e, m * LN2 + jnp.log(l), lse)
            pcat = jnp.concatenate(ps, axis=1)
            vcat = jnp.concatenate([jnp.where(lo, vv, zero), jnp.where(lo, zero, vv)], axis=0)
            o2 = jnp.dot(pcat, vcat, preferred_element_type=F32)
            o_ref[0, 0, rows, lanes] = (o2 * jnp.where(lo, invs[0], invs[1])).astype(o_ref.dtype)
        lse_ref[0, 0, rows, :] = lse
    kbuf[0:blk] = kbuf[nq:nq + blk]
    vbuf[0:blk] = vbuf[nq:nq + blk]


def _attn_core(qkv, window, dilation):
    B, d, n, _ = qkv.shape
    blk = ATTN_BLOCK
    assert d == dilation and window // d == blk and n % ATTN_Q_ROWS == 0
    qi = jnp.arange(blk)[:, None]
    kj = jnp.arange(2 * blk)[None, :]
    dist = qi + blk - kj
    band = (dist >= 0) & (dist <= blk)
    neg = jnp.float32(-jnp.inf)
    bias = jnp.stack([jnp.where(band & (kj >= blk), 0.0, neg), jnp.where(band, 0.0, neg)]).astype(F32)

    def spec(width, part):
        return pl.BlockSpec((1, 1, ATTN_Q_ROWS, width), lambda b, r, i: (b, r, i, part))

    lse_width = ATTN_HEADS * LSE_LANES
    return pl.pallas_call(
        _attn_core_kernel,
        grid=(B, d, n // ATTN_Q_ROWS),
        in_specs=[spec(ATTN_WIDTH, 0), spec(ATTN_WIDTH, 1), spec(ATTN_WIDTH, 2), _resident(bias.shape)],
        out_specs=[spec(ATTN_WIDTH, 0), spec(lse_width, 0)],
        out_shape=[jax.ShapeDtypeStruct((B, d, n, ATTN_WIDTH), BF16),
                   jax.ShapeDtypeStruct((B, d, n, lse_width), F32)],
        scratch_shapes=[pltpu.VMEM((ATTN_Q_ROWS + blk, ATTN_WIDTH), BF16),
                        pltpu.VMEM((ATTN_Q_ROWS + blk, ATTN_WIDTH), BF16)],
        compiler_params=_params(3),
        name=f"attn_core_d{d}",
    )(qkv, qkv, qkv, bias)


def _attn_mix(x_ref, o_refs, l_refs, w_ref, ex_ref, obs, lbs):
    tm = x_ref.shape[1]
    outs = [o_refs[0][0, 0].astype(F32)]
    lses = [l_refs[0][0, 0]]
    for o_ref, l_ref, ob, lb in zip(o_refs[1:], l_refs[1:], obs, lbs):
        d = o_ref.shape[1]
        n_lane_tiles = ob.shape[0]
        for r in range(d):
            o_r = o_ref[0, r].astype(F32)
            for c in range(n_lane_tiles):
                ob[c, pl.ds(r, tm // d, stride=d), :] = o_r[:, c * LANES:(c + 1) * LANES]
            lb[pl.ds(r, tm // d, stride=d), :] = l_ref[0, r]
        outs.append(jnp.concatenate([ob[c] for c in range(n_lane_tiles)], axis=1))
        lses.append(lb[...])
    m = jnp.maximum(jnp.maximum(lses[0], lses[1]), lses[2])
    es = [jnp.exp(l - m) for l in lses]
    inv = 1.0 / (es[0] + es[1] + es[2])
    ex = ex_ref[...]
    mix = None
    for e, o in zip(es, outs):
        w = e * inv
        hi = w.astype(BF16)
        lo = (w - hi.astype(F32)).astype(BF16)
        wide = jnp.dot(hi, ex, preferred_element_type=F32) + jnp.dot(lo, ex, preferred_element_type=F32)
        mix = wide * o if mix is None else mix + wide * o
    return x_ref[0] + jnp.dot(mix.astype(BF16), w_ref[...], preferred_element_type=F32)


def _attn_out_ffn_kernel(x_ref, o0, o1, o2, l0, l1, l2, w_ref, ex_ref, g_ref, wup_ref, cw_ref, cb_ref, wdn_ref,
                         out_ref, ob1, ob2, lb1, lb2, *ffn_scratch, nc, ch):
    x = _attn_mix(x_ref, (o0, o1, o2), (l0, l1, l2), w_ref, ex_ref, (ob1, ob2), (lb1, lb2))
    _ffn_body(x, g_ref, wup_ref, cw_ref, cb_ref, wdn_ref, out_ref, *ffn_scratch, nc=nc, ch=ch)


def _attn_out_ffn(x, outs, lses, w_out, ffn_params):
    B, S, D = x.shape
    assert FFN_ROWS == ROW_TILE and S % ROW_TILE == 0
    lse_width = ATTN_HEADS * LSE_LANES
    tile = pl.BlockSpec((1, ROW_TILE, D), lambda b, i: (b, i, 0))

    def part(width, d):
        return pl.BlockSpec((1, d, ROW_TILE // d, width), lambda b, i: (b, 0, i, 0))

    dil = [d for _, d in ATTN_GROUPS]
    assert dil[0] == 1
    lane = jnp.arange(lse_width)[:, None]
    col = jnp.arange(ATTN_WIDTH)[None, :]
    expand = (lane == (col // ATTN_HEAD_DIM) * LSE_LANES).astype(BF16)
    ffn_args, ffn_specs, ffn_scratch, static = _ffn_operands(*ffn_params)
    wide = pltpu.VMEM((ATTN_WIDTH // LANES, ROW_TILE, LANES), F32)
    narrow = pltpu.VMEM((ROW_TILE, lse_width), F32)
    return pl.pallas_call(
        functools.partial(_attn_out_ffn_kernel, **static),
        grid=(B, S // ROW_TILE),
        in_specs=[tile] + [part(ATTN_WIDTH, d) for d in dil] + [part(lse_width, d) for d in dil]
        + [_resident(w_out.shape), _resident(expand.shape)] + ffn_specs,
        out_specs=tile,
        out_shape=jax.ShapeDtypeStruct(x.shape, x.dtype),
        scratch_shapes=[wide, wide, narrow, narrow] + ffn_scratch,
        compiler_params=_params(2),
        name="attn_out_ffn",
    )(x, *outs, *lses, w_out.astype(BF16), expand, *ffn_args)


def _attention_layer(x, gain, w_in, q_gain, k_gain, w_out, ffn_params):
    qkvs = _attn_in(x, gain, w_in, q_gain, k_gain)
    outs, lses = [], []
    for qkv, (window, dilation) in zip(qkvs, ATTN_GROUPS):
        o, lse = _attn_core(qkv, window, dilation)
        outs.append(o)
        lses.append(lse)
    return _attn_out_ffn(x, outs, lses, w_out, ffn_params)


def _conv_kernel(x_ref, g_ref, win_ref, bin_ref, dw_ref, dwb_ref, lng_ref, lnb_ref, wout_ref, bout_ref,
                 o_ref, gbuf, ybuf, *, width):
    halo = CONV_HALO
    tm = x_ref.shape[1]
    C = gbuf.shape[1]

    @pl.when(pl.program_id(1) == 0)
    def _():
        gbuf[0:halo] = jnp.zeros((halo, C), F32)

    x = x_ref[0]
    hb = _rms_norm(x, g_ref[...]).astype(BF16)
    u = jnp.dot(hb, win_ref[...], preferred_element_type=F32) + bin_ref[...]
    gbuf[halo:] = u[:, :C] * _sigmoid(u[:, C:])

    first = halo - (width - 1)
    rb, lb = CONV_ROW_BLOCK, CONV_LANE_BLOCK
    nwin = rb + halo

    def row_block(i, carry):
        r0 = pl.multiple_of(i * rb, rb)
        for c0 in range(0, C, lb):
            lanes = slice(c0, c0 + lb)
            win = gbuf[pl.ds(r0, nwin), lanes]
            acc = jnp.zeros((rb, lb), F32) + dwb_ref[:, lanes]
            for phase in range(8):
                taps = [j for j in range(width) if (first + j) % 8 == phase]
                if not taps:
                    continue
                shifted = win if phase == 0 else pltpu.roll(win, nwin - phase, 0)
                for j in taps:
                    a8 = (first + j) // 8 * 8
                    acc = acc + dw_ref[j:j + 1, lanes] * shifted[a8:a8 + rb]
            ybuf[pl.ds(r0, rb), lanes] = acc
        return carry

    lax.fori_loop(0, tm // rb, row_block, 0)
    gbuf[0:halo] = gbuf[tm:tm + halo]

    y = ybuf[...]
    mu = jnp.mean(y, axis=-1, keepdims=True)
    yc = y - mu
    var = jnp.mean(yc * yc, axis=-1, keepdims=True)
    z = yc * lax.rsqrt(var + NORM_EPS) * lng_ref[...] + lnb_ref[...]
    z = (z * _sigmoid(z)).astype(BF16)
    o_ref[0] = x + jnp.dot(z, wout_ref[...], preferred_element_type=F32) + bout_ref[...]


def _conv_mixer(x, gain, w_in, b_in, dw_w, dw_b, ln_g, ln_b, w_out, b_out):
    B, S, D = x.shape
    C = w_out.shape[0]
    width = dw_w.shape[0]
    assert width - 1 <= CONV_HALO
    tile = pl.BlockSpec((1, ROW_TILE, D), lambda b, i: (b, i, 0))
    row = lambda v: v.reshape(1, -1)
    args = (x, row(gain), w_in.astype(BF16), row(b_in), dw_w, row(dw_b), row(ln_g), row(ln_b),
            w_out.astype(BF16), row(b_out))
    return pl.pallas_call(
        functools.partial(_conv_kernel, width=width),
        grid=(B, S // ROW_TILE),
        in_specs=[tile] + [_resident(a.shape) for a in args[1:]],
        out_specs=tile,
        out_shape=jax.ShapeDtypeStruct(x.shape, x.dtype),
        scratch_shapes=[pltpu.VMEM((ROW_TILE + CONV_HALO, C), F32), pltpu.VMEM((ROW_TILE, C), F32)],
        compiler_params=_params(2),
        name="conformer_conv",
    )(*args)


def _hgrn_kernel(x_ref, g_ref, win_ref, lbl_ref, ng_ref, wout_ref, o_ref,
                 qbuf, bbuf, cbuf, vbuf, obuf, state, crow, vrow, *, layer):
    ts = x_ref.shape[1]
    KD = qbuf.shape[1]
    hd = HGRN_HEAD_DIM
    ck = HGRN_CHUNK

    @pl.when(pl.program_id(1) == 0)
    def _():
        state[...] = jnp.zeros_like(state)

    lg = lbl_ref[...]
    e = jnp.exp(lg - jnp.max(lg, axis=0, keepdims=True))
    sm = e / jnp.sum(e, axis=0, keepdims=True)
    lb = jnp.zeros((1, KD), F32)
    for l in range(1, layer + 1):
        lb = lb + sm[l:l + 1]

    x = x_ref[0]
    hb = _rms_norm(x, g_ref[...]).astype(BF16)
    proj = lambda p: jnp.dot(hb, win_ref[:, p * KD:(p + 1) * KD], preferred_element_type=F32)
    q = proj(0)
    qbuf[...] = q * _sigmoid(q)
    fz = proj(1)
    log_sig = jnp.minimum(fz, 0.0) - jnp.log(1.0 + jnp.exp(-jnp.abs(fz)))
    log_lb = jnp.log(lb)
    log_1mlb = jnp.log(1.0 - lb)
    gated = log_1mlb + log_sig
    logf = jnp.maximum(log_lb, gated) + jnp.log(1.0 + jnp.exp(-jnp.abs(log_lb - gated)))
    logk = log_1mlb + (log_sig - fz)
    vbuf[...] = proj(2)
    pos = lax.broadcasted_iota(jnp.int32, (ts, 1), 0) % ck
    sh = 1
    while sh < ck:
        logf = logf + jnp.where(pos >= sh, pltpu.roll(logf, sh, 0), 0.0)
        sh *= 2
    b2 = logf * LOG2E
    bbuf[...] = b2
    cbuf[...] = b2 - logk * LOG2E

    half = ck // 2
    row8 = lax.broadcasted_iota(jnp.int32, (half, 1), 0)
    nt = (((1,), (1,)), ((), ()))
    tn = (((0,), (0,)), ((), ()))

    def chunk(c, carry):
        r0 = pl.multiple_of(c * ck, ck)
        rows = pl.ds(r0, ck)
        for h in range(HGRN_HEADS):
            lanes = slice(h * hd, (h + 1) * hd)
            bc = bbuf[rows, lanes]
            qc = qbuf[rows, lanes]
            cc = cbuf[rows, lanes]
            vc = vbuf[rows, lanes]
            crow[h] = cc
            vrow[h] = vc
            st = state[h]
            o = lax.dot_general((qc * jnp.exp2(bc)).astype(BF16), st.astype(BF16), nt,
                                preferred_element_type=F32)
            halves = []
            for t0 in range(0, ck, half):
                bh, qh, oh = bc[t0:t0 + half], qc[t0:t0 + half], o[t0:t0 + half]
                for s in range(t0 + half):
                    src = pl.ds(s, 1)
                    dec = jnp.exp2(bh - crow[h, src, :])
                    if s > t0:
                        dec = jnp.where(row8 >= s - t0, dec, 0.0)
                    sc = jnp.sum(qh * dec, axis=-1, keepdims=True)
                    oh = oh + sc * vrow[h, src, :]
                halves.append(oh)
            obuf[rows, lanes] = jnp.concatenate(halves, axis=0)
            bl = bc[ck - 1:ck]
            kl = jnp.exp2(bl - cc).astype(BF16)
            state[h] = st * jnp.exp2(bl) + lax.dot_general(vc.astype(BF16), kl, tn,
                                                          preferred_element_type=F32)
        return carry

    lax.fori_loop(0, ts // ck, chunk, 0, unroll=2)

    gz = proj(3)
    gz = gz * _sigmoid(gz)
    for h in range(HGRN_HEADS):
        lanes = slice(h * hd, (h + 1) * hd)
        obuf[:, lanes] = _rms_norm(obuf[:, lanes], ng_ref[:, lanes]) * gz[:, lanes]
    o_ref[0] = x + jnp.dot(obuf[...].astype(BF16), wout_ref[...], preferred_element_type=F32)


def _hgrn_mixer(x, gain, w_in, lb_logits, norm_g, w_out, layer):
    B, S, D = x.shape
    KD = HGRN_HEADS * HGRN_HEAD_DIM
    assert w_in.shape[1] == 4 * KD and w_out.shape[0] == KD and S % HGRN_ROWS == 0
    tile = pl.BlockSpec((1, HGRN_ROWS, D), lambda b, i: (b, i, 0))
    buf = pltpu.VMEM((HGRN_ROWS, KD), F32)
    return pl.pallas_call(
        functools.partial(_hgrn_kernel, layer=layer),
        grid=(B, S // HGRN_ROWS),
        in_specs=[tile, _resident((1, D)), _resident(w_in.shape), _resident(lb_logits.shape),
                  _resident((1, KD)), _resident(w_out.shape)],
        out_specs=tile,
        out_shape=jax.ShapeDtypeStruct(x.shape, x.dtype),
        scratch_shapes=[buf, buf, buf, buf, buf,
                        pltpu.VMEM((HGRN_HEADS, HGRN_HEAD_DIM, HGRN_HEAD_DIM), F32),
                        pltpu.VMEM((HGRN_HEADS, HGRN_CHUNK, HGRN_HEAD_DIM), F32),
                        pltpu.VMEM((HGRN_HEADS, HGRN_CHUNK, HGRN_HEAD_DIM), F32)],
        compiler_params=_params(2),
        name="hgrn2",
    )(x, gain.reshape(1, D), w_in.astype(BF16), lb_logits.astype(F32), norm_g.reshape(1, KD),
      w_out.astype(BF16))


def kernel(x, mixer_norm, ffn_norm, attn_w_in, attn_q_gain, attn_k_gain, attn_w_out, conv_w_in, conv_b_in, conv_dw_w, conv_dw_b, conv_ln_g, conv_ln_b, conv_w_out, conv_b_out, hgrn_w_in, hgrn_lb_logits, hgrn_norm_g, hgrn_w_out, ffn_w_up, ffn_conv_w, ffn_conv_b, ffn_w_down):
    depth = mixer_norm.shape[0]
    for layer in range(depth):
        kind = layer % N_MIXERS
        j = layer // N_MIXERS
        ffn_params = (ffn_norm[layer], ffn_w_up[layer], ffn_conv_w[layer], ffn_conv_b[layer],
                      ffn_w_down[layer])
        if kind == 0:
            x = _attention_layer(x, mixer_norm[layer], attn_w_in[j], attn_q_gain[j], attn_k_gain[j],
                                 attn_w_out[j], ffn_params)
            continue
        if kind == 1:
            x = _conv_mixer(x, mixer_norm[layer], conv_w_in[j], conv_b_in[j], conv_dw_w[j], conv_dw_b[j],
                            conv_ln_g[j], conv_ln_b[j], conv_w_out[j], conv_b_out[j])
        else:
            x = _hgrn_mixer(x, mixer_norm[layer], hgrn_w_in[j], hgrn_lb_logits, hgrn_norm_g[j],
                            hgrn_w_out[j], layer)
        x = _ffn(x, *ffn_params)
    return x
```

```python
import functools

import jax
import jax.numpy as jnp
from jax import lax
from jax.experimental import pallas as pl
from jax.experimental.pallas import tpu as pltpu

F32 = jnp.float32
BF16 = jnp.bfloat16

NORM_EPS = 1e-6
N_MIXERS = 3
LANES = 128
LN2 = 0.6931471805599453
LOG2E = 1.4426950408889634

ATTN_GROUPS = ((128, 1), (512, 4), (2048, 16))
ATTN_HEADS = 8
ATTN_HEAD_DIM = 64
ATTN_BLOCK = 128
ATTN_WIDTH = ATTN_HEADS * ATTN_HEAD_DIM
ATTN_PARTS = 3 * len(ATTN_GROUPS)
ATTN_Q_ROWS = 1024
ATTN_IN_ROWS = 1024
LSE_LANES = 16

HGRN_HEADS = 8
HGRN_HEAD_DIM = 128
HGRN_CHUNK = 16
HGRN_ROWS = 1024

CONV_ROW_BLOCK = 64
CONV_LANE_BLOCK = 128
CONV_HALO = 32

FFN_CHUNK = 256
FFN_HALO = 8
FFN_ROWS = 512
ROW_TILE = 512

VMEM_LIMIT = 56 * 1024 * 1024


def _params(n_axes):
    return pltpu.CompilerParams(dimension_semantics=("arbitrary",) * n_axes,
                                vmem_limit_bytes=VMEM_LIMIT)


def _resident(shape):
    zeros = (0,) * len(shape)
    return pl.BlockSpec(shape, lambda *_: zeros, pipeline_mode=pl.Buffered(1))


def _rms_norm(x, gain):
    ms = jnp.mean(x * x, axis=-1, keepdims=True)
    return x * lax.rsqrt(ms + NORM_EPS) * gain


def _sigmoid(x):
    return 1.0 / (1.0 + jnp.exp2(x * -LOG2E))


def _shift_rows(u, prev, k):
    r = pltpu.roll(u, k, 0)
    p = pltpu.roll(prev, k, 0)
    row = lax.broadcasted_iota(jnp.int32, prev.shape, 0)
    head = jnp.where(row < k, p, r[:8])
    return jnp.concatenate([head, r[8:]], axis=0)


def _ffn_body(x, g_ref, wup_ref, cw_ref, cb_ref, wdn_ref, o_ref, halo_ref, *, nc, ch):
    @pl.when(pl.program_id(1) == 0)
    def _():
        halo_ref[...] = jnp.zeros_like(halo_ref)

    tm = x.shape[0]
    hb = _rms_norm(x, g_ref[...]).astype(BF16)
    u = jnp.dot(hb, wup_ref[...], preferred_element_type=F32)
    parts = []
    for c in range(nc):
        uc = u[:, c * 2 * ch:(c + 1) * 2 * ch]
        prev = halo_ref[c]
        cw = cw_ref[c]
        y = (uc * cw[2:3] + _shift_rows(uc, prev, 1) * cw[1:2] + _shift_rows(uc, prev, 2) * cw[0:1]
             + cb_ref[c])
        halo_ref[c] = uc[tm - FFN_HALO:]
        gate = y[:, :ch]
        parts.append((gate * _sigmoid(gate) * y[:, ch:]).astype(BF16))
    act = jnp.concatenate(parts, axis=1)
    o_ref[0] = x + jnp.dot(act, wdn_ref[...], preferred_element_type=F32)


def _ffn_kernel(x_ref, *refs, nc, ch):
    _ffn_body(x_ref[0], *refs, nc=nc, ch=ch)


def _pack_w_up_kernel(gate_ref, up_ref, o_ref):
    ch = gate_ref.shape[1]
    o_ref[:, :ch] = gate_ref[...].astype(BF16)
    o_ref[:, ch:] = up_ref[...].astype(BF16)


def _pack_w_up(w_up, nc, ch):
    D = w_up.shape[0]
    return pl.pallas_call(
        _pack_w_up_kernel,
        grid=(nc,),
        in_specs=[pl.BlockSpec((D, ch), lambda c: (0, c)), pl.BlockSpec((D, ch), lambda c: (0, nc + c))],
        out_specs=pl.BlockSpec((D, 2 * ch), lambda c: (0, c)),
        out_shape=jax.ShapeDtypeStruct(w_up.shape, BF16),
        compiler_params=_params(1),
        name="pack_w_up",
    )(w_up, w_up)


def _ffn_operands(gain, w_up, conv_w, conv_b, w_down):
    D = w_up.shape[0]
    F = w_down.shape[0]
    ch = FFN_CHUNK
    nc = F // ch
    assert nc * ch == F and nc >= 2
    wup = _pack_w_up(w_up, nc, ch)
    cw = conv_w.reshape(3, 2, nc, ch).transpose(2, 0, 1, 3).reshape(nc, 3, 2 * ch)
    cb = conv_b.reshape(2, nc, ch).transpose(1, 0, 2).reshape(nc, 1, 2 * ch)
    args = (gain.reshape(1, D), wup, cw, cb, w_down.astype(BF16))
    scratch = [pltpu.VMEM((nc, FFN_HALO, 2 * ch), F32)]
    return args, [_resident(a.shape) for a in args], scratch, dict(nc=nc, ch=ch)


def _ffn(x, gain, w_up, conv_w, conv_b, w_down):
    B, S, D = x.shape
    assert S % FFN_ROWS == 0
    args, specs, scratch, static = _ffn_operands(gain, w_up, conv_w, conv_b, w_down)
    tile = pl.BlockSpec((1, FFN_ROWS, D), lambda b, i: (b, i, 0))
    return pl.pallas_call(
        functools.partial(_ffn_kernel, **static),
        grid=(B, S // FFN_ROWS),
        in_specs=[tile] + specs,
        out_specs=tile,
        out_shape=jax.ShapeDtypeStruct(x.shape, x.dtype),
        scratch_shapes=scratch,
        compiler_params=_params(2),
        name="ffn",
    )(x, *args)


def _attn_in_kernel(x_ref, g_ref, w_ref, qk_ref, bd_ref, o0_ref, o1_ref, o2_ref, ybuf):
    outs = (o0_ref, o1_ref, o2_ref)
    tm = x_ref.shape[1]
    hb = _rms_norm(x_ref[0], g_ref[...]).astype(BF16)
    bd = bd_ref[...]
    half = bd.shape[0]
    n_lane_tiles = ATTN_WIDTH // LANES
    for j in range(ATTN_PARTS):
        g, part = divmod(j, 3)
        d = ATTN_GROUPS[g][1]
        acc = jnp.dot(hb, w_ref[:, j * ATTN_WIDTH:(j + 1) * ATTN_WIDTH], preferred_element_type=F32)
        if part != 2:
            sq = (acc * acc).astype(BF16)
            ssq = jnp.concatenate(
                [jnp.dot(sq[:, :half], bd, preferred_element_type=F32),
                 jnp.dot(sq[:, half:], bd, preferred_element_type=F32)], axis=1)
            acc = acc * lax.rsqrt(ssq * (1.0 / ATTN_HEAD_DIM) + NORM_EPS) * qk_ref[j:j + 1, :]
        cols = slice(part * ATTN_WIDTH, (part + 1) * ATTN_WIDTH)
        if d == 1:
            outs[g][0, 0, :, cols] = acc.astype(BF16)
        else:
            for c in range(n_lane_tiles):
                ybuf[c] = acc[:, c * LANES:(c + 1) * LANES]
            for r in range(d):
                piece = [ybuf[c, pl.ds(r, tm // d, stride=d), :] for c in range(n_lane_tiles)]
                outs[g][0, r, :, cols] = jnp.concatenate(piece, axis=1).astype(BF16)


def _attn_in(x, gain, w_in, q_gain, k_gain):
    B, S, D = x.shape
    W = w_in.shape[1]
    scale = ATTN_HEAD_DIM ** -0.5 * LOG2E
    rows = []
    for g in range(len(ATTN_GROUPS)):
        rows += [jnp.tile(q_gain[g], ATTN_HEADS) * scale, jnp.tile(k_gain[g], ATTN_HEADS),
                 jnp.ones((ATTN_WIDTH,), F32)]
    qk = jnp.stack(rows).astype(F32)
    half = ATTN_WIDTH // 2
    head = jnp.arange(half) // ATTN_HEAD_DIM
    bd = (head[:, None] == head[None, :]).astype(BF16)
    out_specs, out_shape = [], []
    tile_rows = ATTN_IN_ROWS
    assert S % tile_rows == 0
    for _, d in ATTN_GROUPS:
        assert tile_rows % (16 * d) == 0
        out_specs.append(pl.BlockSpec((1, d, tile_rows // d, 3 * ATTN_WIDTH), lambda b, i: (b, 0, i, 0)))
        out_shape.append(jax.ShapeDtypeStruct((B, d, S // d, 3 * ATTN_WIDTH), BF16))
    return pl.pallas_call(
        _attn_in_kernel,
        grid=(B, S // tile_rows),
        in_specs=[pl.BlockSpec((1, tile_rows, D), lambda b, i: (b, i, 0)), _resident((1, D)),
                  _resident((D, W)), _resident(qk.shape), _resident(bd.shape)],
        out_specs=out_specs,
        out_shape=out_shape,
        scratch_shapes=[pltpu.VMEM((ATTN_WIDTH // LANES, tile_rows, LANES), F32)],
        compiler_params=_params(2),
        name="attn_in",
    )(x, gain.reshape(1, D), w_in.astype(BF16), qk, bd)


def _attn_core_kernel(q_ref, k_ref, v_ref, bias_ref, o_ref, lse_ref, kbuf, vbuf):
    blk = ATTN_BLOCK
    nq = q_ref.shape[2]
    first = pl.program_id(2) == 0

    @pl.when(first)
    def _():
        kbuf[0:blk] = jnp.zeros((blk, ATTN_WIDTH), BF16)
        vbuf[0:blk] = jnp.zeros((blk, ATTN_WIDTH), BF16)

    kbuf[blk:] = k_ref[0, 0]
    vbuf[blk:] = v_ref[0, 0]
    lo = lax.broadcasted_iota(jnp.int32, (1, 2 * ATTN_HEAD_DIM), 1) < ATTN_HEAD_DIM
    lse_head = lax.broadcasted_iota(jnp.int32, (1, ATTN_HEADS * LSE_LANES), 1) // LSE_LANES
    zero = jnp.zeros((), BF16)
    nt = (((1,), (1,)), ((), ()))
    for jq in range(nq // blk):
        bias = bias_ref[1]
        if jq == 0:
            bias = jnp.where(first, bias_ref[0], bias)
        rows = slice(jq * blk, (jq + 1) * blk)
        lse = jnp.zeros((blk, ATTN_HEADS * LSE_LANES), F32)
        for hp in range(ATTN_HEADS // 2):
            lanes = slice(hp * 2 * ATTN_HEAD_DIM, (hp + 1) * 2 * ATTN_HEAD_DIM)
            q2 = q_ref[0, 0, rows, lanes]
            kk = kbuf[jq * blk:(jq + 2) * blk, lanes]
            vv = vbuf[jq * blk:(jq + 2) * blk, lanes]
            qq = jnp.concatenate([jnp.where(lo, q2, zero), jnp.where(lo, zero, q2)], axis=0)
            s2 = lax.dot_general(qq, kk, nt, preferred_element_type=F32)
            ps, invs = [], []
            for e in range(2):
                s = s2[e * blk:(e + 1) * blk] + bias
                m = jnp.max(s, axis=-1, keepdims=True)
                p = jnp.exp2(s - m)
                l = jnp.sum(p, axis=-1, keepdims=True)
                ps.append(p.astype(BF16))
                invs.append(1.0 / l)
                lse = jnp.where(lse_head == 2 * hp + e, m * LN2 + jnp.log(l), lse)
            pcat = jnp.concatenate(ps, axis=1)
            vcat = jnp.concatenate([jnp.where(lo, vv, zero), jnp.where(lo, zero, vv)], axis=0)
            o2 = jnp.dot(pcat, vcat, preferred_element_type=F32)
            o_ref[0, 0, rows, lanes] = (o2 * jnp.where(lo, invs[0], invs[1])).astype(o_ref.dtype)
        lse_ref[0, 0, rows, :] = lse
    kbuf[0:blk] = kbuf[nq:nq + blk]
    vbuf[0:blk] = vbuf[nq:nq + blk]


def _attn_core(qkv, window, dilation):
    B, d, n, _ = qkv.shape
    blk = ATTN_BLOCK
    assert d == dilation and window // d == blk and n % ATTN_Q_ROWS == 0
    qi = jnp.arange(blk)[:, None]
    kj = jnp.arange(2 * blk)[None, :]
    dist = qi + blk - kj
    band = (dist >= 0) & (dist <= blk)
    neg = jnp.float32(-jnp.inf)
    bias = jnp.stack([jnp.where(band & (kj >= blk), 0.0, neg), jnp.where(band, 0.0, neg)]).astype(F32)

    def spec(width, part):
        return pl.BlockSpec((1, 1, ATTN_Q_ROWS, width), lambda b, r, i: (b, r, i, part))

    lse_width = ATTN_HEADS * LSE_LANES
    return pl.pallas_call(
        _attn_core_kernel,
        grid=(B, d, n // ATTN_Q_ROWS),
        in_specs=[spec(ATTN_WIDTH, 0), spec(ATTN_WIDTH, 1), spec(ATTN_WIDTH, 2), _resident(bias.shape)],
        out_specs=[spec(ATTN_WIDTH, 0), spec(lse_width, 0)],
        out_shape=[jax.ShapeDtypeStruct((B, d, n, ATTN_WIDTH), BF16),
                   jax.ShapeDtypeStruct((B, d, n, lse_width), F32)],
        scratch_shapes=[pltpu.VMEM((ATTN_Q_ROWS + blk, ATTN_WIDTH), BF16),
                        pltpu.VMEM((ATTN_Q_ROWS + blk, ATTN_WIDTH), BF16)],
        compiler_params=_params(3),
        name=f"attn_core_d{d}",
    )(qkv, qkv, qkv, bias)


def _attn_mix(x_ref, o_refs, l_refs, w_ref, ex_ref, obs, lbs):
    tm = x_ref.shape[1]
    outs = [o_refs[0][0, 0].astype(F32)]
    lses = [l_refs[0][0, 0]]
    for o_ref, l_ref, ob, lb in zip(o_refs[1:], l_refs[1:], obs, lbs):
        d = o_ref.shape[1]
        n_lane_tiles = ob.shape[0]
        for r in range(d):
            o_r = o_ref[0, r].astype(F32)
            for c in range(n_lane_tiles):
                ob[c, pl.ds(r, tm // d, stride=d), :] = o_r[:, c * LANES:(c + 1) * LANES]
            lb[pl.ds(r, tm // d, stride=d), :] = l_ref[0, r]
        outs.append(jnp.concatenate([ob[c] for c in range(n_lane_tiles)], axis=1))
        lses.append(lb[...])
    m = jnp.maximum(jnp.maximum(lses[0], lses[1]), lses[2])
    es = [jnp.exp(l - m) for l in lses]
    inv = 1.0 / (es[0] + es[1] + es[2])
    ex = ex_ref[...]
    mix = None
    for e, o in zip(es, outs):
        w = e * inv
        hi = w.astype(BF16)
        lo = (w - hi.astype(F32)).astype(BF16)
        wide = jnp.dot(hi, ex, preferred_element_type=F32) + jnp.dot(lo, ex, preferred_element_type=F32)
        mix = wide * o if mix is None else mix + wide * o
    return x_ref[0] + jnp.dot(mix.astype(BF16), w_ref[...], preferred_element_type=F32)


def _attn_out_ffn_kernel(x_ref, o0, o1, o2, l0, l1, l2, w_ref, ex_ref, g_ref, wup_ref, cw_ref, cb_ref, wdn_ref,
                         out_ref, ob1, ob2, lb1, lb2, *ffn_scratch, nc, ch):
    x = _attn_mix(x_ref, (o0, o1, o2), (l0, l1, l2), w_ref, ex_ref, (ob1, ob2), (lb1, lb2))
    _ffn_body(x, g_ref, wup_ref, cw_ref, cb_ref, wdn_ref, out_ref, *ffn_scratch, nc=nc, ch=ch)


def _attn_out_ffn(x, outs, lses, w_out, ffn_params):
    B, S, D = x.shape
    assert FFN_ROWS == ROW_TILE and S % ROW_TILE == 0
    lse_width = ATTN_HEADS * LSE_LANES
    tile = pl.BlockSpec((1, ROW_TILE, D), lambda b, i: (b, i, 0))

    def part(width, d):
        return pl.BlockSpec((1, d, ROW_TILE // d, width), lambda b, i: (b, 0, i, 0))

    dil = [d for _, d in ATTN_GROUPS]
    assert dil[0] == 1
    lane = jnp.arange(lse_width)[:, None]
    col = jnp.arange(ATTN_WIDTH)[None, :]
    expand = (lane == (col // ATTN_HEAD_DIM) * LSE_LANES).astype(BF16)
    ffn_args, ffn_specs, ffn_scratch, static = _ffn_operands(*ffn_params)
    wide = pltpu.VMEM((ATTN_WIDTH // LANES, ROW_TILE, LANES), F32)
    narrow = pltpu.VMEM((ROW_TILE, lse_width), F32)
    return pl.pallas_call(
        functools.partial(_attn_out_ffn_kernel, **static),
        grid=(B, S // ROW_TILE),
        in_specs=[tile] + [part(ATTN_WIDTH, d) for d in dil] + [part(lse_width, d) for d in dil]
        + [_resident(w_out.shape), _resident(expand.shape)] + ffn_specs,
        out_specs=tile,
        out_shape=jax.ShapeDtypeStruct(x.shape, x.dtype),
        scratch_shapes=[wide, wide, narrow, narrow] + ffn_scratch,
        compiler_params=_params(2),
        name="attn_out_ffn",
    )(x, *outs, *lses, w_out.astype(BF16), expand, *ffn_args)


def _attention_layer(x, gain, w_in, q_gain, k_gain, w_out, ffn_params):
    qkvs = _attn_in(x, gain, w_in, q_gain, k_gain)
    outs, lses = [], []
    for qkv, (window, dilation) in zip(qkvs, ATTN_GROUPS):
        o, lse = _attn_core(qkv, window, dilation)
        outs.append(o)
        lses.append(lse)
    return _attn_out_ffn(x, outs, lses, w_out, ffn_params)


def _conv_kernel(x_ref, g_ref, win_ref, bin_ref, dw_ref, dwb_ref, lng_ref, lnb_ref, wout_ref, bout_ref,
                 o_ref, gbuf, ybuf, *, width):
    halo = CONV_HALO
    tm = x_ref.shape[1]
    C = gbuf.shape[1]

    @pl.when(pl.program_id(1) == 0)
    def _():
        gbuf[0:halo] = jnp.zeros((halo, C), F32)

    x = x_ref[0]
    hb = _rms_norm(x, g_ref[...]).astype(BF16)
    u = jnp.dot(hb, win_ref[...], preferred_element_type=F32) + bin_ref[...]
    gbuf[halo:] = u[:, :C] * _sigmoid(u[:, C:])

    first = halo - (width - 1)
    rb, lb = CONV_ROW_BLOCK, CONV_LANE_BLOCK
    nwin = rb + halo

    def row_block(i, carry):
        r0 = pl.multiple_of(i * rb, rb)
        for c0 in range(0, C, lb):
            lanes = slice(c0, c0 + lb)
            win = gbuf[pl.ds(r0, nwin), lanes]
            acc = jnp.zeros((rb, lb), F32) + dwb_ref[:, lanes]
            for phase in range(8):
                taps = [j for j in range(width) if (first + j) % 8 == phase]
                if not taps:
                    continue
                shifted = win if phase == 0 else pltpu.roll(win, nwin - phase, 0)
                for j in taps:
                    a8 = (first + j) // 8 * 8
                    acc = acc + dw_ref[j:j + 1, lanes] * shifted[a8:a8 + rb]
            ybuf[pl.ds(r0, rb), lanes] = acc
        return carry

    lax.fori_loop(0, tm // rb, row_block, 0)
    gbuf[0:halo] = gbuf[tm:tm + halo]

    y = ybuf[...]
    mu = jnp.mean(y, axis=-1, keepdims=True)
    yc = y - mu
    var = jnp.mean(yc * yc, axis=-1, keepdims=True)
    z = yc * lax.rsqrt(var + NORM_EPS) * lng_ref[...] + lnb_ref[...]
    z = (z * _sigmoid(z)).astype(BF16)
    o_ref[0] = x + jnp.dot(z, wout_ref[...], preferred_element_type=F32) + bout_ref[...]


def _conv_mixer(x, gain, w_in, b_in, dw_w, dw_b, ln_g, ln_b, w_out, b_out):
    B, S, D = x.shape
    C = w_out.shape[0]
    width = dw_w.shape[0]
    assert width - 1 <= CONV_HALO
    tile = pl.BlockSpec((1, ROW_TILE, D), lambda b, i: (b, i, 0))
    row = lambda v: v.reshape(1, -1)
    args = (x, row(gain), w_in.astype(BF16), row(b_in), dw_w, row(dw_b), row(ln_g), row(ln_b),
            w_out.astype(BF16), row(b_out))
    return pl.pallas_call(
        functools.partial(_conv_kernel, width=width),
        grid=(B, S // ROW_TILE),
        in_specs=[tile] + [_resident(a.shape) for a in args[1:]],
        out_specs=tile,
        out_shape=jax.ShapeDtypeStruct(x.shape, x.dtype),
        scratch_shapes=[pltpu.VMEM((ROW_TILE + CONV_HALO, C), F32), pltpu.VMEM((ROW_TILE, C), F32)],
        compiler_params=_params(2),
        name="conformer_conv",
    )(*args)


def _hgrn_kernel(x_ref, g_ref, win_ref, lbl_ref, ng_ref, wout_ref, o_ref,
                 qbuf, bbuf, cbuf, vbuf, obuf, state, crow, vrow, *, layer):
    ts = x_ref.shape[1]
    KD = qbuf.shape[1]
    hd = HGRN_HEAD_DIM
    ck = HGRN_CHUNK

    @pl.when(pl.program_id(1) == 0)
    def _():
        state[...] = jnp.zeros_like(state)

    lg = lbl_ref[...]
    e = jnp.exp(lg - jnp.max(lg, axis=0, keepdims=True))
    sm = e / jnp.sum(e, axis=0, keepdims=True)
    lb = jnp.zeros((1, KD), F32)
    for l in range(1, layer + 1):
        lb = lb + sm[l:l + 1]

    x = x_ref[0]
    hb = _rms_norm(x, g_ref[...]).astype(BF16)
    proj = lambda p: jnp.dot(hb, win_ref[:, p * KD:(p + 1) * KD], preferred_element_type=F32)
    q = proj(0)
    qbuf[...] = q * _sigmoid(q)
    fz = proj(1)
    log_sig = jnp.minimum(fz, 0.0) - jnp.log(1.0 + jnp.exp(-jnp.abs(fz)))
    log_lb = jnp.log(lb)
    log_1mlb = jnp.log(1.0 - lb)
    gated = log_1mlb + log_sig
    logf = jnp.maximum(log_lb, gated) + jnp.log(1.0 + jnp.exp(-jnp.abs(log_lb - gated)))
    logk = log_1mlb + (log_sig - fz)
    vbuf[...] = proj(2)
    pos = lax.broadcasted_iota(jnp.int32, (ts, 1), 0) % ck
    sh = 1
    while sh < ck:
        logf = logf + jnp.where(pos >= sh, pltpu.roll(logf, sh, 0), 0.0)
        sh *= 2
    b2 = logf * LOG2E
    bbuf[...] = b2
    cbuf[...] = b2 - logk * LOG2E

    half = ck // 2
    row8 = lax.broadcasted_iota(jnp.int32, (half, 1), 0)
    nt = (((1,), (1,)), ((), ()))
    tn = (((0,), (0,)), ((), ()))

    def chunk(c, carry):
        r0 = pl.multiple_of(c * ck, ck)
        rows = pl.ds(r0, ck)
        for h in range(HGRN_HEADS):
            lanes = slice(h * hd, (h + 1) * hd)
            bc = bbuf[rows, lanes]
            qc = qbuf[rows, lanes]
            cc = cbuf[rows, lanes]
            vc = vbuf[rows, lanes]
            crow[h] = cc
            vrow[h] = vc
            st = state[h]
            o = lax.dot_general((qc * jnp.exp2(bc)).astype(BF16), st.astype(BF16), nt,
                                preferred_element_type=F32)
            halves = []
            for t0 in range(0, ck, half):
                bh, qh, oh = bc[t0:t0 + half], qc[t0:t0 + half], o[t0:t0 + half]
                for s in range(t0 + half):
                    src = pl.ds(s, 1)
                    dec = jnp.exp2(bh - crow[h, src, :])
                    if s > t0:
                        dec = jnp.where(row8 >= s - t0, dec, 0.0)
                    sc = jnp.sum(qh * dec, axis=-1, keepdims=True)
                    oh = oh + sc * vrow[h, src, :]
                halves.append(oh)
            obuf[rows, lanes] = jnp.concatenate(halves, axis=0)
            bl = bc[ck - 1:ck]
            kl = jnp.exp2(bl - cc).astype(BF16)
            state[h] = st * jnp.exp2(bl) + lax.dot_general(vc.astype(BF16), kl, tn,
                                                          preferred_element_type=F32)
        return carry

    lax.fori_loop(0, ts // ck, chunk, 0, unroll=2)

    gz = proj(3)
    gz = gz * _sigmoid(gz)
    for h in range(HGRN_HEADS):
        lanes = slice(h * hd, (h + 1) * hd)
        obuf[:, lanes] = _rms_norm(obuf[:, lanes], ng_ref[:, lanes]) * gz[:, lanes]
    o_ref[0] = x + jnp.dot(obuf[...].astype(BF16), wout_ref[...], preferred_element_type=F32)


def _hgrn_mixer(x, gain, w_in, lb_logits, norm_g, w_out, layer):
    B, S, D = x.shape
    KD = HGRN_HEADS * HGRN_HEAD_DIM
    assert w_in.shape[1] == 4 * KD and w_out.shape[0] == KD and S % HGRN_ROWS == 0
    tile = pl.BlockSpec((1, HGRN_ROWS, D), lambda b, i: (b, i, 0))
    buf = pltpu.VMEM((HGRN_ROWS, KD), F32)
    return pl.pallas_call(
        functools.partial(_hgrn_kernel, layer=layer),
        grid=(B, S // HGRN_ROWS),
        in_specs=[tile, _resident((1, D)), _resident(w_in.shape), _resident(lb_logits.shape),
                  _resident((1, KD)), _resident(w_out.shape)],
        out_specs=tile,
        out_shape=jax.ShapeDtypeStruct(x.shape, x.dtype),
        scratch_shapes=[buf, buf, buf, buf, buf,
                        pltpu.VMEM((HGRN_HEADS, HGRN_HEAD_DIM, HGRN_HEAD_DIM), F32),
                        pltpu.VMEM((HGRN_HEADS, HGRN_CHUNK, HGRN_HEAD_DIM), F32),
                        pltpu.VMEM((HGRN_HEADS, HGRN_CHUNK, HGRN_HEAD_DIM), F32)],
        compiler_params=_params(2),
        name="hgrn2",
    )(x, gain.reshape(1, D), w_in.astype(BF16), lb_logits.astype(F32), norm_g.reshape(1, KD),
      w_out.astype(BF16))


def kernel(x, mixer_norm, ffn_norm, attn_w_in, attn_q_gain, attn_k_gain, attn_w_out, conv_w_in, conv_b_in, conv_dw_w, conv_dw_b, conv_ln_g, conv_ln_b, conv_w_out, conv_b_out, hgrn_w_in, hgrn_lb_logits, hgrn_norm_g, hgrn_w_out, ffn_w_up, ffn_conv_w, ffn_conv_b, ffn_w_down):
    depth = mixer_norm.shape[0]
    for layer in range(depth):
        kind = layer % N_MIXERS
        j = layer // N_MIXERS
        ffn_params = (ffn_norm[layer], ffn_w_up[layer], ffn_conv_w[layer], ffn_conv_b[layer],
                      ffn_w_down[layer])
        if kind == 0:
            x = _attention_layer(x, mixer_norm[layer], attn_w_in[j], attn_q_gain[j], attn_k_gain[j],
                                 attn_w_out[j], ffn_params)
            continue
        if kind == 1:
            x = _conv_mixer(x, mixer_norm[layer], conv_w_in[j], conv_b_in[j], conv_dw_w[j], conv_dw_b[j],
                            conv_ln_g[j], conv_ln_b[j], conv_w_out[j], conv_b_out[j])
        else:
            x = _hgrn_mixer(x, mixer_norm[layer], hgrn_w_in[j], hgrn_lb_logits, hgrn_norm_g[j],
                            hgrn_w_out[j], layer)
        x = _ffn(x, *ffn_params)
    return x
```
